```python
import math
import jax, jax.numpy as jnp
from jax import lax
import numpy as np

D_MODEL = 1024
BATCH = 8
SEQ = 8192
DEPTH = 2
DEC_BATCH = 4
DEC_SEQ = 8192
PAST_LEN = 128

GRID_W = 64
Q_BLOCK = 128
ROPE_THETA = 10000.0
EPS = 1e-6
DIFF_HEADS = 4
DIFF_HEAD_DIM = 64
DIFF_V_DIM = 2 * DIFF_HEAD_DIM
DIFF_WIDTH = DIFF_HEADS * DIFF_V_DIM
GQA_HEADS = 8
GQA_KV_HEADS = 2
GQA_GROUP = GQA_HEADS // GQA_KV_HEADS
GQA_HEAD_DIM = 64
GQA_AXIS_DIM = GQA_HEAD_DIM // 2
GQA_WIDTH = GQA_HEADS * GQA_HEAD_DIM
N_BRANCH = 2
D_FF = 2816

SPLIT_SIZES = [
    DIFF_HEADS * 2 * DIFF_HEAD_DIM,
    DIFF_HEADS * 2 * DIFF_HEAD_DIM,
    DIFF_WIDTH,
    GQA_HEADS * GQA_HEAD_DIM,
    GQA_KV_HEADS * GQA_HEAD_DIM,
    GQA_KV_HEADS * GQA_HEAD_DIM,
    N_BRANCH * D_MODEL,
]
SPLIT_POINTS = [int(v) for v in np.cumsum(SPLIT_SIZES)[:-1]]
D_IN = int(sum(SPLIT_SIZES))

kernel_name = "hybrid_diffattn_axial_gqa_macaron_encoder"


def _rmsnorm(x, g):
    xf = x.astype(jnp.float32)
    y = xf * lax.rsqrt(jnp.mean(xf * xf, axis=-1, keepdims=True) + EPS)
    return (y * g.astype(jnp.float32)).astype(x.dtype)


def _inv_freq(dim):
    return ROPE_THETA ** (-jnp.arange(0, dim, 2, dtype=jnp.float32) / dim)


def _rope(x, ang):
    m = ang.shape[-1]
    shape = (ang.shape[0],) + (1,) * (x.ndim - 3) + (m,)
    cos = jnp.cos(ang).reshape(shape).astype(x.dtype)
    sin = jnp.sin(ang).reshape(shape).astype(x.dtype)
    x1, x2 = x[..., :m], x[..., m:]
    return jnp.concatenate([x1 * cos - x2 * sin, x1 * sin + x2 * cos], axis=-1)


def _axial_rope(x, ang_row, ang_col):
    return jnp.concatenate([_rope(x[..., :GQA_AXIS_DIM], ang_row),
                            _rope(x[..., GQA_AXIS_DIM:], ang_col)], axis=-1)


def _positions(S):
    rows = S // GRID_W
    row = jnp.broadcast_to(jnp.arange(rows)[:, None], (rows, GRID_W)).reshape(S).astype(jnp.float32)
    col = jnp.broadcast_to(jnp.arange(GRID_W)[None, :], (rows, GRID_W)).reshape(S).astype(jnp.float32)
    t = jnp.arange(S).astype(jnp.float32)
    ang_t = t[:, None] * _inv_freq(DIFF_HEAD_DIM)[None, :]
    ang_row = row[:, None] * _inv_freq(GQA_AXIS_DIM)[None, :]
    ang_col = col[:, None] * _inv_freq(GQA_AXIS_DIM)[None, :]
    return ang_t, ang_row, ang_col


def _swiglu(h, w_gu, w_down):
    g, u = jnp.split(h @ w_gu, 2, axis=-1)
    return (jax.nn.silu(g) * u) @ w_down


def _diff_attention(q, k, v, lam):
    B, S = q.shape[0], q.shape[1]
    nb = S // Q_BLOCK
    scale = 1.0 / math.sqrt(DIFF_HEAD_DIM)
    qb = jnp.moveaxis(q.reshape(B, nb, Q_BLOCK, DIFF_HEADS, 2, DIFF_HEAD_DIM), 1, 0)

    def one(qblk):
        s = jnp.einsum('bqhcd,bkhcd->bhcqk', qblk, k).astype(jnp.float32) * scale
        p = jax.nn.softmax(s, axis=-1)
        p = p[:, :, 0] - lam * p[:, :, 1]
        return jnp.einsum('bhqk,bkhe->bqhe', p.astype(v.dtype), v)

    o = lax.map(one, qb)
    return jnp.moveaxis(o, 0, 1).reshape(B, S, DIFF_HEADS, DIFF_V_DIM)


def _gqa_attention(q, k, v):
    B, S = q.shape[0], q.shape[1]
    nb = S // Q_BLOCK
    scale = 1.0 / math.sqrt(GQA_HEAD_DIM)
    qb = jnp.moveaxis(q.reshape(B, nb, Q_BLOCK, GQA_KV_HEADS, GQA_GROUP, GQA_HEAD_DIM), 1, 0)

    def one(qblk):
        s = jnp.einsum('bqgrd,bkgd->bgrqk', qblk, k).astype(jnp.float32) * scale
        p = jax.nn.softmax(s, axis=-1).astype(v.dtype)
        return jnp.einsum('bgrqk,bkgd->bqgrd', p, v)

    o = lax.map(one, qb)
    return jnp.moveaxis(o, 0, 1).reshape(B, S, GQA_WIDTH)


def _layer(x, pos, lam_init, ffn1_norm, ffn1_w_gu, ffn1_w_down, mix_norm, w_in,
           diff_q_norm, diff_k_norm, lambda_q1, lambda_k1, lambda_q2, lambda_k2, diff_subln,
           gqa_q_norm, gqa_k_norm, w_up_diff, w_up_gqa, w_out, ffn2_norm, ffn2_w_gu, ffn2_w_down):
    ang_t, ang_row, ang_col = pos
    B, S = x.shape[0], x.shape[1]
    x = x + 0.5 * _swiglu(_rmsnorm(x, ffn1_norm), ffn1_w_gu, ffn1_w_down)

    h = _rmsnorm(x, mix_norm)
    u = h @ w_in
    dq, dk, dv, gq, gk, gv, gates = jnp.split(u, SPLIT_POINTS, axis=-1)

    dq = dq.reshape(B, S, DIFF_HEADS, 2, DIFF_HEAD_DIM)
    dk = dk.reshape(B, S, DIFF_HEADS, 2, DIFF_HEAD_DIM)
    dv = dv.reshape(B, S, DIFF_HEADS, DIFF_V_DIM)
    dq = _rope(_rmsnorm(dq, diff_q_norm), ang_t)
    dk = _rope(_rmsnorm(dk, diff_k_norm), ang_t)
    f32 = jnp.float32
    lam = (jnp.exp(jnp.sum(lambda_q1.astype(f32) * lambda_k1.astype(f32)))
           - jnp.exp(jnp.sum(lambda_q2.astype(f32) * lambda_k2.astype(f32))) + lam_init)
    a = _diff_attention(dq, dk, dv, lam)
    a = _rmsnorm(a, diff_subln) * (1.0 - lam_init)
    a = a.reshape(B, S, DIFF_WIDTH) @ w_up_diff

    gq = gq.reshape(B, S, GQA_KV_HEADS, GQA_GROUP, GQA_HEAD_DIM)
    gk = gk.reshape(B, S, GQA_KV_HEADS, GQA_HEAD_DIM)
    gv = gv.reshape(B, S, GQA_KV_HEADS, GQA_HEAD_DIM)
    gq = _axial_rope(_rmsnorm(gq, gqa_q_norm), ang_row, ang_col)
    gk = _axial_rope(_rmsnorm(gk, gqa_k_norm), ang_row, ang_col)
    b = _gqa_attention(gq, gk, gv) @ w_up_gqa

    g = jax.nn.sigmoid(gates.reshape(B, S, N_BRANCH, D_MODEL))
    merged = g[:, :, 0] * a + g[:, :, 1] * b
    x = x + merged @ w_out

    x = x + 0.5 * _swiglu(_rmsnorm(x, ffn2_norm), ffn2_w_gu, ffn2_w_down)
    return x


def setup_inputs(seed: int = 0) -> dict:
    key = jax.random.key(seed)
    ks = jax.random.split(key, 24)

    def w(k, shape, fan_in):
        return jax.random.normal(k, shape, jnp.float32) * (fan_in ** -0.5)

    def gain(k, shape):
        return 1.0 + 0.02 * jax.random.normal(k, shape, jnp.float32)

    def small(k, shape):
        return 0.1 * jax.random.normal(k, shape, jnp.float32)

    L, D = DEPTH, D_MODEL
    return {
        "x_prompt": jax.random.normal(ks[0], (BATCH, SEQ, D), jnp.float32),
        "x_sample": jax.random.normal(ks[1], (DEC_BATCH, DEC_SEQ, D), jnp.float32),
        "ffn1_norm": gain(ks[2], (L, D)),
        "ffn1_w_gu": w(ks[3], (L, D, 2 * D_FF), D),
        "ffn1_w_down": w(ks[4], (L, D_FF, D), D_FF),
        "mix_norm": gain(ks[5], (L, D)),
        "w_in": w(ks[6], (L, D, D_IN), D),
        "diff_q_norm": gain(ks[7], (L, DIFF_HEAD_DIM)),
        "diff_k_norm": gain(ks[8], (L, DIFF_HEAD_DIM)),
        "lambda_q1": small(ks[9], (L, DIFF_HEAD_DIM)),
        "lambda_k1": small(ks[10], (L, DIFF_HEAD_DIM)),
        "lambda_q2": small(ks[11], (L, DIFF_HEAD_DIM)),
        "lambda_k2": small(ks[12], (L, DIFF_HEAD_DIM)),
        "diff_subln": gain(ks[13], (L, DIFF_V_DIM)),
        "gqa_q_norm": gain(ks[14], (L, GQA_HEAD_DIM)),
        "gqa_k_norm": gain(ks[15], (L, GQA_HEAD_DIM)),
        "w_up_diff": w(ks[16], (L, DIFF_WIDTH, D), DIFF_WIDTH),
        "w_up_gqa": w(ks[17], (L, GQA_WIDTH, D), GQA_WIDTH),
        "w_out": w(ks[18], (L, D, D), D),
        "ffn2_norm": gain(ks[19], (L, D)),
        "ffn2_w_gu": w(ks[20], (L, D, 2 * D_FF), D),
        "ffn2_w_down": w(ks[21], (L, D_FF, D), D_FF),
    }


def reference(x_prompt, x_sample, ffn1_norm, ffn1_w_gu, ffn1_w_down, mix_norm, w_in,
              diff_q_norm, diff_k_norm, lambda_q1, lambda_k1, lambda_q2, lambda_k2, diff_subln,
              gqa_q_norm, gqa_k_norm, w_up_diff, w_up_gqa, w_out, ffn2_norm, ffn2_w_gu, ffn2_w_down):
    pos_prompt = _positions(x_prompt.shape[1])
    pos_sample = _positions(x_sample.shape[1])
    y_prompt = x_prompt
    y_sample = x_sample
    for l in range(DEPTH):
        lam_init = 0.8 - 0.6 * math.exp(-0.3 * l)
        params = (ffn1_norm[l], ffn1_w_gu[l], ffn1_w_down[l], mix_norm[l], w_in[l],
                  diff_q_norm[l], diff_k_norm[l], lambda_q1[l], lambda_k1[l], lambda_q2[l], lambda_k2[l],
                  diff_subln[l], gqa_q_norm[l], gqa_k_norm[l], w_up_diff[l], w_up_gqa[l], w_out[l],
                  ffn2_norm[l], ffn2_w_gu[l], ffn2_w_down[l])
        y_prompt = _layer(y_prompt, pos_prompt, lam_init, *params)
        y_sample = _layer(y_sample, pos_sample, lam_init, *params)
    return (y_prompt, y_sample)
```

```python
import functools
import math

import jax
import jax.numpy as jnp
from jax import lax
from jax.experimental import pallas as pl
from jax.experimental.pallas import tpu as pltpu

GRID_W = 64
ROPE_THETA = 10000.0
EPS = 1e-6
DIFF_HEADS = 4
DIFF_HEAD_DIM = 64
DIFF_V_DIM = 2 * DIFF_HEAD_DIM
DIFF_WIDTH = DIFF_HEADS * DIFF_V_DIM
GQA_HEADS = 8
GQA_KV_HEADS = 2
GQA_GROUP = GQA_HEADS // GQA_KV_HEADS
GQA_HEAD_DIM = 64
GQA_AXIS_DIM = GQA_HEAD_DIM // 2
GQA_WIDTH = GQA_HEADS * GQA_HEAD_DIM
HEAD = 64

SEC_DQ = 0
SEC_DK = SEC_DQ + DIFF_HEADS * 2 * DIFF_HEAD_DIM
SEC_DV = SEC_DK + DIFF_HEADS * 2 * DIFF_HEAD_DIM
SEC_GQ = SEC_DV + DIFF_WIDTH
SEC_GK = SEC_GQ + GQA_WIDTH
SEC_GV = SEC_GK + GQA_KV_HEADS * GQA_HEAD_DIM
SEC_GATES = SEC_GV + GQA_KV_HEADS * GQA_HEAD_DIM

VMEM_LIMIT_BYTES = 56 * 1024 * 1024

BF16 = jnp.bfloat16
F32 = jnp.float32


def _pick_tile(n, pref):
    t = min(n, pref)
    while n % t:
        t //= 2
    return t


def _params(*sem):
    return pltpu.CompilerParams(dimension_semantics=sem, vmem_limit_bytes=VMEM_LIMIT_BYTES)


def _resident(shape):
    nd = len(shape)
    return pl.BlockSpec(shape, lambda *_: (0,) * nd, pipeline_mode=pl.Buffered(1))


def _dot(a, b):
    return jnp.dot(a, b, preferred_element_type=F32)


def _rms_scale(x):
    return lax.rsqrt(jnp.mean(x * x, axis=0, keepdims=True) + EPS)


def _ffn_kernel(x_ref, g_ref, wg_ref, wu_ref, wd_ref, o_ref, *, in_natural, out_natural):
    x = x_ref[0]
    if in_natural:
        x = x.T
    h = (x * _rms_scale(x) * g_ref[...]).astype(BF16)
    gate = _dot(wg_ref[...], h)
    up = _dot(wu_ref[...], h)
    act = (gate * (1.0 / (1.0 + jnp.exp(-gate))) * up).astype(BF16)
    y = x + 0.5 * _dot(wd_ref[...], act)
    o_ref[0] = y.T if out_natural else y


def _ffn(x, g, wgT, wuT, wdT, *, in_natural=False, out_natural=False, ts_pref=512):
    if in_natural:
        B, S, D = x.shape
    else:
        B, D, S = x.shape
    F = wgT.shape[0]
    ts = _pick_tile(S, ts_pref)
    nat = pl.BlockSpec((1, ts, D), lambda b, i: (b, i, 0))
    tr = pl.BlockSpec((1, D, ts), lambda b, i: (b, 0, i))
    out_shape = (B, S, D) if out_natural else (B, D, S)
    return pl.pallas_call(
        functools.partial(_ffn_kernel, in_natural=in_natural, out_natural=out_natural),
        grid=(B, S // ts),
        in_specs=[nat if in_natural else tr, _resident((D, 1)),
                  _resident((F, D)), _resident((F, D)), _resident((D, F))],
        out_specs=nat if out_natural else tr,
        out_shape=jax.ShapeDtypeStruct(out_shape, F32),
        compiler_params=_params("parallel", "parallel"),
        name="ffn",
    )(x, g, wgT, wuT, wdT)


def _swap_diff(x):
    return jnp.concatenate([x[HEAD // 2:], x[:HEAD // 2]], axis=0)


def _swap_axial(x):
    q = HEAD // 4
    return jnp.concatenate([x[q:2 * q], x[:q], x[3 * q:], x[2 * q:3 * q]], axis=0)


def _norm_rope(u, n_groups, tab_ref, swap):
    ta = tab_ref[0]
    tb = tab_ref[1]
    outs = []
    for j in range(n_groups):
        x = u[j * HEAD:(j + 1) * HEAD]
        outs.append(_rms_scale(x) * (x * ta + swap(x) * tb))
    return jnp.concatenate(outs, axis=0)


def _mix_in_kernel(x_ref, g_ref, w_ref, tdq_ref, tdk_ref, tgq_ref, tgk_ref,
                   dq_ref, dk_ref, dv_ref, gq_ref, gk_ref, gv_ref, gates_ref):
    x = x_ref[0]
    h = (x * _rms_scale(x) * g_ref[...]).astype(BF16)

    def proj(lo, hi):
        return _dot(w_ref[lo:hi, :], h)

    n_dqk = DIFF_HEADS * 2
    dq_ref[0] = _norm_rope(proj(SEC_DQ, SEC_DK), n_dqk, tdq_ref, _swap_diff).astype(BF16)
    dk_ref[0] = _norm_rope(proj(SEC_DK, SEC_DV), n_dqk, tdk_ref, _swap_diff).T.astype(BF16)
    dv_ref[0] = proj(SEC_DV, SEC_GQ).astype(BF16)
    gq_ref[0] = _norm_rope(proj(SEC_GQ, SEC_GK), GQA_HEADS, tgq_ref, _swap_axial).astype(BF16)
    gk_ref[0] = _norm_rope(proj(SEC_GK, SEC_GV), GQA_KV_HEADS, tgk_ref, _swap_axial).T.astype(BF16)
    gv_ref[0] = proj(SEC_GV, SEC_GATES).astype(BF16)
    z = proj(SEC_GATES, w_ref.shape[0])
    gates_ref[0] = (1.0 / (1.0 + jnp.exp(-z))).astype(BF16)


def _mix_in(xT, g, w_inT, tabs, *, ts_pref=512):
    B, D, S = xT.shape
    d_in = w_inT.shape[0]
    ts = _pick_tile(S, ts_pref)
    n_gates = d_in - SEC_GATES
    kvw = GQA_KV_HEADS * GQA_HEAD_DIM

    def tr(rows):
        return pl.BlockSpec((1, rows, ts), lambda b, i: (b, 0, i))

    def nat(cols):
        return pl.BlockSpec((1, ts, cols), lambda b, i: (b, i, 0))

    tab = pl.BlockSpec((2, HEAD, ts), lambda b, i: (0, 0, i))
    sds = jax.ShapeDtypeStruct
    return pl.pallas_call(
        _mix_in_kernel,
        grid=(B, S // ts),
        in_specs=[tr(D), _resident((D, 1)), _resident((d_in, D)), tab, tab, tab, tab],
        out_specs=[tr(SEC_DK - SEC_DQ), nat(SEC_DV - SEC_DK), tr(DIFF_WIDTH),
                   tr(GQA_WIDTH), nat(kvw), tr(kvw), tr(n_gates)],
        out_shape=[sds((B, SEC_DK - SEC_DQ, S), BF16), sds((B, S, SEC_DV - SEC_DK), BF16),
                   sds((B, DIFF_WIDTH, S), BF16), sds((B, GQA_WIDTH, S), BF16),
                   sds((B, S, kvw), BF16), sds((B, kvw, S), BF16), sds((B, n_gates, S), BF16)],
        compiler_params=_params("parallel", "parallel"),
        name="mix_in",
    )(xT, g, w_inT, *tabs)


def _online_softmax_loop(k_ref, qpad, pv_update, acc_refs, n_cols, tk):
    S = k_ref.shape[1]
    for a in acc_refs:
        a[...] = jnp.zeros_like(a)

    def body(kt, carry):
        m, l = carry
        k0 = pl.multiple_of(kt * tk, tk)
        s = _dot(k_ref[0, pl.ds(k0, tk), :], qpad)
        m_new = jnp.maximum(m, jnp.max(s, axis=0, keepdims=True))
        alpha = jnp.exp(m - m_new)
        p = jnp.exp(s - m_new)
        l = alpha * l + jnp.sum(p, axis=0, keepdims=True)
        pv_update(k0, p.astype(BF16), alpha)
        return m_new, l

    m0 = jnp.full((1, n_cols), -jnp.inf, F32)
    l0 = jnp.zeros((1, n_cols), F32)
    _, l = lax.fori_loop(0, S // tk, body, (m0, l0))
    return l


def _diff_attn_kernel(q_ref, k_ref, v_ref, lam_ref, sub_ref, o_ref, acc_ref, *, tk, lam_init):
    tq = q_ref.shape[2]
    q = q_ref[0]
    zeros = jnp.zeros((HEAD, tq), BF16)
    qpad = jnp.concatenate(
        [jnp.concatenate([q[:HEAD], zeros], axis=1),
         jnp.concatenate([zeros, q[HEAD:]], axis=1)], axis=0)

    def pv_update(k0, p, alpha):
        acc_ref[...] = alpha * acc_ref[...] + _dot(v_ref[0, :, pl.ds(k0, tk)], p)

    l = _online_softmax_loop(k_ref, qpad, pv_update, [acc_ref], 2 * tq, tk)
    o = acc_ref[...] / l
    lam_p = lam_ref[...]
    lam = (jnp.exp(jnp.sum(lam_p[0:1] * lam_p[1:2], axis=1, keepdims=True))
           - jnp.exp(jnp.sum(lam_p[2:3] * lam_p[3:4], axis=1, keepdims=True)) + lam_init)
    d = o[:, :tq] - lam * o[:, tq:]
    o_ref[0] = (d * _rms_scale(d) * sub_ref[...] * (1.0 - lam_init)).astype(BF16)


def _diff_attn(dqT, dk, dvT, lam_p, sub_g, lam_init, *, tq_pref=512, tk_pref=256):
    B, _, S = dqT.shape
    tq = _pick_tile(S, tq_pref)
    tk = _pick_tile(S, tk_pref)
    hv = DIFF_V_DIM
    return pl.pallas_call(
        functools.partial(_diff_attn_kernel, tk=tk, lam_init=lam_init),
        grid=(B, DIFF_HEADS, S // tq),
        in_specs=[pl.BlockSpec((1, 2 * DIFF_HEAD_DIM, tq), lambda b, h, i: (b, h, i)),
                  pl.BlockSpec((1, S, 2 * DIFF_HEAD_DIM), lambda b, h, i: (b, 0, h)),
                  pl.BlockSpec((1, hv, S), lambda b, h, i: (b, h, 0)),
                  _resident(lam_p.shape), _resident(sub_g.shape)],
        out_specs=pl.BlockSpec((1, hv, tq), lambda b, h, i: (b, h, i)),
        out_shape=jax.ShapeDtypeStruct((B, DIFF_WIDTH, S), BF16),
        scratch_shapes=[pltpu.VMEM((hv, 2 * tq), F32)],
        compiler_params=_params("parallel", "parallel", "parallel"),
        name="diff_attn",
    )(dqT, dk, dvT, lam_p, sub_g)


def _gqa_attn_kernel(q_ref, k_ref, v_ref, o_ref, acc0_ref, acc1_ref, *, tk):
    tq = q_ref.shape[2]
    q = q_ref[0]
    gw = GQA_GROUP * tq
    zeros = jnp.zeros((GQA_HEAD_DIM, gw), BF16)

    def heads_on_lanes(kv):
        return jnp.concatenate(
            [q[(kv * GQA_GROUP + r) * GQA_HEAD_DIM:(kv * GQA_GROUP + r + 1) * GQA_HEAD_DIM]
             for r in range(GQA_GROUP)], axis=1)

    qpad = jnp.concatenate(
        [jnp.concatenate([heads_on_lanes(0), zeros], axis=1),
         jnp.concatenate([zeros, heads_on_lanes(1)], axis=1)], axis=0)
    accs = [acc0_ref, acc1_ref]

    def pv_update(k0, p, alpha):
        v = v_ref[0, :, pl.ds(k0, tk)]
        for kv in range(GQA_KV_HEADS):
            lo = kv * gw
            accs[kv][...] = (alpha[:, lo:lo + gw] * accs[kv][...]
                             + _dot(v[kv * GQA_HEAD_DIM:(kv + 1) * GQA_HEAD_DIM], p[:, lo:lo + gw]))

    l = _online_softmax_loop(k_ref, qpad, pv_update, accs, GQA_HEADS * tq, tk)
    for kv in range(GQA_KV_HEADS):
        o = accs[kv][...] / l[:, kv * gw:(kv + 1) * gw]
        for r in range(GQA_GROUP):
            j = kv * GQA_GROUP + r
            o_ref[0, j * GQA_HEAD_DIM:(j + 1) * GQA_HEAD_DIM, :] = o[:, r * tq:(r + 1) * tq].astype(BF16)


def _gqa_attn(gqT, gk, gvT, *, tq_pref=256, tk_pref=256):
    B, _, S = gqT.shape
    tq = _pick_tile(S, tq_pref)
    tk = _pick_tile(S, tk_pref)
    kvw = GQA_KV_HEADS * GQA_HEAD_DIM
    return pl.pallas_call(
        functools.partial(_gqa_attn_kernel, tk=tk),
        grid=(B, S // tq),
        in_specs=[pl.BlockSpec((1, GQA_WIDTH, tq), lambda b, i: (b, 0, i)),
                  pl.BlockSpec((1, S, kvw), lambda b, i: (b, 0, 0)),
                  pl.BlockSpec((1, kvw, S), lambda b, i: (b, 0, 0))],
        out_specs=pl.BlockSpec((1, GQA_WIDTH, tq), lambda b, i: (b, 0, i)),
        out_shape=jax.ShapeDtypeStruct((B, GQA_WIDTH, S), BF16),
        scratch_shapes=[pltpu.VMEM((GQA_HEAD_DIM, GQA_GROUP * tq), F32)] * GQA_KV_HEADS,
        compiler_params=_params("parallel", "parallel"),
        name="gqa_attn",
    )(gqT, gk, gvT)


def _mix_out_kernel(x_ref, a_ref, b_ref, gates_ref, wa_ref, wb_ref, wo_ref, o_ref):
    D = x_ref.shape[1]
    a = _dot(wa_ref[...], a_ref[0])
    b = _dot(wb_ref[...], b_ref[0])
    merged = gates_ref[0, :D, :].astype(F32) * a + gates_ref[0, D:, :].astype(F32) * b
    o_ref[0] = x_ref[0] + _dot(wo_ref[...], merged.astype(BF16))


def _mix_out(xT, aT, bT, gatesT, waT, wbT, woT, *, ts_pref=512):
    B, D, S = xT.shape
    ts = _pick_tile(S, ts_pref)

    def tr(rows):
        return pl.BlockSpec((1, rows, ts), lambda b, i: (b, 0, i))

    return pl.pallas_call(
        _mix_out_kernel,
        grid=(B, S // ts),
        in_specs=[tr(D), tr(aT.shape[1]), tr(bT.shape[1]), tr(gatesT.shape[1]),
                  _resident(waT.shape), _resident(wbT.shape), _resident(woT.shape)],
        out_specs=tr(D),
        out_shape=jax.ShapeDtypeStruct((B, D, S), F32),
        compiler_params=_params("parallel", "parallel"),
        name="mix_out",
    )(xT, aT, bT, gatesT, waT, wbT, woT)


def _inv_freq(dim):
    return ROPE_THETA ** (-jnp.arange(0, dim, 2, dtype=F32) / dim)


def _rope_tables(S):
    t = jnp.arange(S, dtype=F32)
    row = jnp.floor_divide(jnp.arange(S), GRID_W).astype(F32)
    col = jnp.remainder(jnp.arange(S), GRID_W).astype(F32)
    ang_t = (t[:, None] * _inv_freq(DIFF_HEAD_DIM)[None, :]).T
    ang_r = (row[:, None] * _inv_freq(GQA_AXIS_DIM)[None, :]).T
    ang_c = (col[:, None] * _inv_freq(GQA_AXIS_DIM)[None, :]).T
    cos_d = jnp.concatenate([jnp.cos(ang_t)] * 2, axis=0)
    sin_d = jnp.concatenate([-jnp.sin(ang_t), jnp.sin(ang_t)], axis=0)
    cos_a = jnp.concatenate([jnp.cos(ang_r)] * 2 + [jnp.cos(ang_c)] * 2, axis=0)
    sin_a = jnp.concatenate([-jnp.sin(ang_r), jnp.sin(ang_r), -jnp.sin(ang_c), jnp.sin(ang_c)], axis=0)
    return (cos_d, sin_d), (cos_a, sin_a)


def _gain_table(gain, cos_sin, swap, scale):
    cos, sin = cos_sin
    g = gain.astype(F32)[:, None] * scale
    return jnp.stack([g * cos, swap(g) * sin], axis=0)


def _layer(xT, S, p, lam_init, ropes, first, last):
    (ffn1_norm, ffn1_w_gu, ffn1_w_down, mix_norm, w_in, diff_q_norm, diff_k_norm,
     lambda_q1, lambda_k1, lambda_q2, lambda_k2, diff_subln, gqa_q_norm, gqa_k_norm,
     w_up_diff, w_up_gqa, w_out, ffn2_norm, ffn2_w_gu, ffn2_w_down) = p
    rope_d, rope_a = ropes

    def col(v):
        return v.astype(F32)[:, None]

    def ffn_weights(w_gu, w_down):
        f = w_down.shape[0]
        return (w_gu[:, :f].T.astype(BF16), w_gu[:, f:].T.astype(BF16), w_down.T.astype(BF16))

    xT = _ffn(xT, col(ffn1_norm), *ffn_weights(ffn1_w_gu, ffn1_w_down), in_natural=first)

    tabs = (_gain_table(diff_q_norm, rope_d, _swap_diff, 1.0 / math.sqrt(DIFF_HEAD_DIM)),
            _gain_table(diff_k_norm, rope_d, _swap_diff, 1.0),
            _gain_table(gqa_q_norm, rope_a, _swap_axial, 1.0 / math.sqrt(GQA_HEAD_DIM)),
            _gain_table(gqa_k_norm, rope_a, _swap_axial, 1.0))
    dqT, dk, dvT, gqT, gk, gvT, gatesT = _mix_in(xT, col(mix_norm), w_in.T.astype(BF16), tabs)

    lam_p = jnp.stack([lambda_q1, lambda_k1, lambda_q2, lambda_k2], axis=0).astype(F32)
    aT = _diff_attn(dqT, dk, dvT, lam_p, col(diff_subln), lam_init)
    bT = _gqa_attn(gqT, gk, gvT)
    xT = _mix_out(xT, aT, bT, gatesT, w_up_diff.T.astype(BF16), w_up_gqa.T.astype(BF16),
                  w_out.T.astype(BF16))
    return _ffn(xT, col(ffn2_norm), *ffn_weights(ffn2_w_gu, ffn2_w_down), out_natural=last)


def kernel(x_prompt, x_sample, ffn1_norm, ffn1_w_gu, ffn1_w_down, mix_norm, w_in, diff_q_norm, diff_k_norm, lambda_q1, lambda_k1, lambda_q2, lambda_k2, diff_subln, gqa_q_norm, gqa_k_norm, w_up_diff, w_up_gqa, w_out, ffn2_norm, ffn2_w_gu, ffn2_w_down):
    stacked = (ffn1_norm, ffn1_w_gu, ffn1_w_down, mix_norm, w_in, diff_q_norm, diff_k_norm,
               lambda_q1, lambda_k1, lambda_q2, lambda_k2, diff_subln, gqa_q_norm, gqa_k_norm,
               w_up_diff, w_up_gqa, w_out, ffn2_norm, ffn2_w_gu, ffn2_w_down)
    depth = ffn1_norm.shape[0]
    outs = []
    for x in (x_prompt, x_sample):
        S = x.shape[1]
        ropes = _rope_tables(S)
        y = x
        for l in range(depth):
            lam_init = 0.8 - 0.6 * math.exp(-0.3 * l)
            y = _layer(y, S, tuple(w[l] for w in stacked), lam_init, ropes,
                       first=(l == 0), last=(l == depth - 1))
        outs.append(y)
    return tuple(outs)
```

```python
import functools
import math

import jax
import jax.numpy as jnp
from jax import lax
from jax.experimental import pallas as pl
from jax.experimental.pallas import tpu as pltpu

GRID_W = 64
ROPE_THETA = 10000.0
EPS = 1e-6
DIFF_HEADS = 4
DIFF_HEAD_DIM = 64
DIFF_V_DIM = 2 * DIFF_HEAD_DIM
DIFF_WIDTH = DIFF_HEADS * DIFF_V_DIM
GQA_HEADS = 8
GQA_KV_HEADS = 2
GQA_GROUP = GQA_HEADS // GQA_KV_HEADS
GQA_HEAD_DIM = 64
GQA_AXIS_DIM = GQA_HEAD_DIM // 2
GQA_WIDTH = GQA_HEADS * GQA_HEAD_DIM
HEAD = 64
LOG2E = math.log2(math.e)

SEC_DQ = 0
SEC_DK = SEC_DQ + DIFF_HEADS * 2 * DIFF_HEAD_DIM
SEC_DV = SEC_DK + DIFF_HEADS * 2 * DIFF_HEAD_DIM
SEC_GQ = SEC_DV + DIFF_WIDTH
SEC_GK = SEC_GQ + GQA_WIDTH
SEC_GV = SEC_GK + GQA_KV_HEADS * GQA_HEAD_DIM
SEC_GATES = SEC_GV + GQA_KV_HEADS * GQA_HEAD_DIM

VMEM_LIMIT_BYTES = 56 * 1024 * 1024

BF16 = jnp.bfloat16
F32 = jnp.float32


def _pick_tile(n, pref):
    t = min(n, pref)
    while n % t:
        t //= 2
    return t


def _params(*sem):
    return pltpu.CompilerParams(dimension_semantics=sem, vmem_limit_bytes=VMEM_LIMIT_BYTES)


def _resident(shape):
    nd = len(shape)
    return pl.BlockSpec(shape, lambda *_: (0,) * nd, pipeline_mode=pl.Buffered(1))


def _dot(a, b):
    return jnp.dot(a, b, preferred_element_type=F32)


def _rms_scale(x):
    return lax.rsqrt(jnp.mean(x * x, axis=0, keepdims=True) + EPS)


def _ffn_kernel(x_ref, g_ref, wg_ref, wu_ref, wd_ref, o_ref, *, in_natural, out_natural):
    x = x_ref[0]
    if in_natural:
        x = x.T
    h = (x * _rms_scale(x) * g_ref[...]).astype(BF16)
    gate = _dot(wg_ref[...], h)
    up = _dot(wu_ref[...], h)
    act = (gate * (1.0 / (1.0 + jnp.exp(-gate))) * up).astype(BF16)
    y = x + 0.5 * _dot(wd_ref[...], act)
    o_ref[0] = y.T if out_natural else y


def _ffn(x, g, wgT, wuT, wdT, *, in_natural=False, out_natural=False, ts_pref=512):
    if in_natural:
        B, S, D = x.shape
    else:
        B, D, S = x.shape
    F = wgT.shape[0]
    ts = _pick_tile(S, ts_pref)
    nat = pl.BlockSpec((1, ts, D), lambda b, i: (b, i, 0))
    tr = pl.BlockSpec((1, D, ts), lambda b, i: (b, 0, i))
    out_shape = (B, S, D) if out_natural else (B, D, S)
    return pl.pallas_call(
        functools.partial(_ffn_kernel, in_natural=in_natural, out_natural=out_natural),
        grid=(B, S // ts),
        in_specs=[nat if in_natural else tr, _resident((D, 1)),
                  _resident((F, D)), _resident((F, D)), _resident((D, F))],
        out_specs=nat if out_natural else tr,
        out_shape=jax.ShapeDtypeStruct(out_shape, F32),
        compiler_params=_params("parallel", "parallel"),
        name="ffn",
    )(x, g, wgT, wuT, wdT)


def _swap_diff(x):
    return jnp.concatenate([x[HEAD // 2:], x[:HEAD // 2]], axis=0)


def _swap_axial(x):
    q = HEAD // 4
    return jnp.concatenate([x[q:2 * q], x[:q], x[3 * q:], x[2 * q:3 * q]], axis=0)


def _norm_rope(u, n_groups, tab_ref, swap):
    ta = tab_ref[0]
    tb = tab_ref[1]
    outs = []
    for j in range(n_groups):
        x = u[j * HEAD:(j + 1) * HEAD]
        outs.append(_rms_scale(x) * (x * ta + swap(x) * tb))
    return jnp.concatenate(outs, axis=0)


def _mix_in_kernel(x_ref, g_ref, w_ref, tdq_ref, tdk_ref, tgq_ref, tgk_ref,
                   dq_ref, dk_ref, dv_ref, gq_ref, gk_ref, gv_ref, gates_ref):
    x = x_ref[0]
    h = (x * _rms_scale(x) * g_ref[...]).astype(BF16)

    def proj(lo, hi):
        return _dot(w_ref[lo:hi, :], h)

    n_dqk = DIFF_HEADS * 2
    dq_ref[0] = _norm_rope(proj(SEC_DQ, SEC_DK), n_dqk, tdq_ref, _swap_diff).astype(BF16)
    dk_ref[0] = _norm_rope(proj(SEC_DK, SEC_DV), n_dqk, tdk_ref, _swap_diff).T.astype(BF16)
    dv_ref[0] = proj(SEC_DV, SEC_GQ).astype(BF16)
    gq_ref[0] = _norm_rope(proj(SEC_GQ, SEC_GK), GQA_HEADS, tgq_ref, _swap_axial).astype(BF16)
    gk_ref[0] = _norm_rope(proj(SEC_GK, SEC_GV), GQA_KV_HEADS, tgk_ref, _swap_axial).T.astype(BF16)
    gv_ref[0] = proj(SEC_GV, SEC_GATES).astype(BF16)
    z = proj(SEC_GATES, w_ref.shape[0])
    gates_ref[0] = (1.0 / (1.0 + jnp.exp(-z))).astype(BF16)


def _mix_in(xT, g, w_inT, tabs, *, ts_pref=512):
    B, D, S = xT.shape
    d_in = w_inT.shape[0]
    ts = _pick_tile(S, ts_pref)
    n_gates = d_in - SEC_GATES
    kvw = GQA_KV_HEADS * GQA_HEAD_DIM

    def tr(rows):
        return pl.BlockSpec((1, rows, ts), lambda b, i: (b, 0, i))

    def nat(cols):
        return pl.BlockSpec((1, ts, cols), lambda b, i: (b, i, 0))

    tab = pl.BlockSpec((2, HEAD, ts), lambda b, i: (0, 0, i))
    sds = jax.ShapeDtypeStruct
    return pl.pallas_call(
        _mix_in_kernel,
        grid=(B, S // ts),
        in_specs=[tr(D), _resident((D, 1)), _resident((d_in, D)), tab, tab, tab, tab],
        out_specs=[tr(SEC_DK - SEC_DQ), nat(SEC_DV - SEC_DK), tr(DIFF_WIDTH),
                   tr(GQA_WIDTH), nat(kvw), tr(kvw), tr(n_gates)],
        out_shape=[sds((B, SEC_DK - SEC_DQ, S), BF16), sds((B, S, SEC_DV - SEC_DK), BF16),
                   sds((B, DIFF_WIDTH, S), BF16), sds((B, GQA_WIDTH, S), BF16),
                   sds((B, S, kvw), BF16), sds((B, kvw, S), BF16), sds((B, n_gates, S), BF16)],
        compiler_params=_params("parallel", "parallel"),
        name="mix_in",
    )(xT, g, w_inT, *tabs)


ATTN_COLS = 512
SUM_ROWS = 16


def _flash_loop(k_ref, v_ref, qs_ref, s_ref, acc_ref, v_rows, tk):
    n_tiles = k_ref.shape[1] // tk
    assert n_tiles % 2 == 0
    G = qs_ref.shape[0]
    cols = qs_ref.shape[2]
    acc_ref[...] = jnp.zeros_like(acc_ref)
    ones_rows = jnp.ones((SUM_ROWS, tk), BF16)

    def scores(kt, buf, g):
        k0 = pl.multiple_of(kt * tk, tk)
        s = _dot(k_ref[0, pl.ds(k0, tk), :], qs_ref[g])
        s_ref[buf, g] = s
        return jnp.max(s, axis=0, keepdims=True)

    def step(kt, buf, carry):
        ms, tile_max = carry
        k0 = pl.multiple_of(kt * tk, tk)
        vblk = v_ref[0, :, pl.ds(k0, tk)]
        kt_next = jnp.minimum(kt + 1, n_tiles - 1)
        new_m, next_max = [], []
        for g in range(G):
            next_max.append(scores(kt_next, 1 - buf, g))
            m_new = jnp.maximum(ms[g], tile_max[g])
            alpha = jnp.exp2(ms[g] - m_new)
            p = jnp.exp2(s_ref[buf, g] - m_new).astype(BF16)
            lo, hi = v_rows[g]
            v_ext = jnp.concatenate([vblk[lo:hi], ones_rows], axis=0)
            acc_ref[g] = alpha * acc_ref[g] + _dot(v_ext, p)
            new_m.append(m_new)
        return tuple(new_m), tuple(next_max)

    def body(u, carry):
        return step(2 * u + 1, 1, step(2 * u, 0, carry))

    m0 = tuple(jnp.full((1, cols), -jnp.inf, F32) for _ in range(G))
    max0 = tuple(scores(0, 0, g) for g in range(G))
    lax.fori_loop(0, n_tiles // 2, body, (m0, max0))


def _normalized(acc_ref, g, dv):
    return acc_ref[g, :dv, :] / acc_ref[g, dv:dv + 1, :]


def _diff_attn_kernel(q_ref, k_ref, v_ref, lam_ref, sub_ref, o_ref, qs_ref, s_ref, acc_ref, *,
                      tk, lam_init):
    tq = q_ref.shape[2]
    cols = qs_ref.shape[2]
    nb = tq // cols
    q = q_ref[0]
    first_half = lax.broadcasted_iota(jnp.int32, q.shape, 0) < HEAD
    zero = jnp.zeros_like(q)
    q_maps = (jnp.where(first_half, q, zero), jnp.where(first_half, zero, q))
    for mp in range(2):
        for cblk in range(nb):
            qs_ref[mp * nb + cblk] = q_maps[mp][:, cblk * cols:(cblk + 1) * cols]

    dv = v_ref.shape[1]
    _flash_loop(k_ref, v_ref, qs_ref, s_ref, acc_ref, [(0, dv)] * (2 * nb), tk)

    lam_p = lam_ref[...]
    lam = (jnp.exp(jnp.sum(lam_p[0:1] * lam_p[1:2], axis=1, keepdims=True))
           - jnp.exp(jnp.sum(lam_p[2:3] * lam_p[3:4], axis=1, keepdims=True)) + lam_init)
    for cblk in range(nb):
        d = _normalized(acc_ref, cblk, dv) - lam * _normalized(acc_ref, nb + cblk, dv)
        o_ref[0, :, cblk * cols:(cblk + 1) * cols] = (
            d * _rms_scale(d) * sub_ref[...] * (1.0 - lam_init)).astype(BF16)


def _diff_attn(dqT, dk, dvT, lam_p, sub_g, lam_init, *, tq_pref=512, tk_pref=512):
    B, _, S = dqT.shape
    tq = _pick_tile(S, tq_pref)
    tk = _pick_tile(S, tk_pref)
    cols = _pick_tile(tq, ATTN_COLS)
    hv = DIFF_V_DIM
    n_chains = 2 * (tq // cols)
    return pl.pallas_call(
        functools.partial(_diff_attn_kernel, tk=tk, lam_init=lam_init),
        grid=(B, DIFF_HEADS, S // tq),
        in_specs=[pl.BlockSpec((1, 2 * DIFF_HEAD_DIM, tq), lambda b, h, i: (b, h, i)),
                  pl.BlockSpec((1, S, 2 * DIFF_HEAD_DIM), lambda b, h, i: (b, 0, h)),
                  pl.BlockSpec((1, hv, S), lambda b, h, i: (b, h, 0)),
                  _resident(lam_p.shape), _resident(sub_g.shape)],
        out_specs=pl.BlockSpec((1, hv, tq), lambda b, h, i: (b, h, i)),
        out_shape=jax.ShapeDtypeStruct((B, DIFF_WIDTH, S), BF16),
        scratch_shapes=[pltpu.VMEM((n_chains, 2 * DIFF_HEAD_DIM, cols), BF16),
                        pltpu.VMEM((2, n_chains, tk, cols), F32),
                        pltpu.VMEM((n_chains, hv + SUM_ROWS, cols), F32)],
        compiler_params=_params("parallel", "parallel", "parallel"),
        name="diff_attn",
    )(dqT, dk, dvT, lam_p, sub_g)


def _gqa_attn_kernel(q_ref, k_ref, v_ref, o_ref, qs_ref, s_ref, acc_ref, *, tk):
    d = GQA_HEAD_DIM
    tq = q_ref.shape[2]
    hp = qs_ref.shape[2] // tq
    zeros = jnp.zeros((d, hp * tq), BF16)
    v_rows = []
    for g in range(GQA_HEADS // hp):
        kv = (g * hp) // GQA_GROUP
        qh = jnp.concatenate([q_ref[0, j * d:(j + 1) * d, :] for j in range(g * hp, (g + 1) * hp)], axis=1)
        qs_ref[g] = jnp.concatenate([qh, zeros] if kv == 0 else [zeros, qh], axis=0)
        v_rows.append((kv * d, (kv + 1) * d))

    _flash_loop(k_ref, v_ref, qs_ref, s_ref, acc_ref, v_rows, tk)
    for g in range(GQA_HEADS // hp):
        o = _normalized(acc_ref, g, d)
        for r in range(hp):
            j = g * hp + r
            o_ref[0, j * d:(j + 1) * d, :] = o[:, r * tq:(r + 1) * tq].astype(BF16)


def _gqa_attn(gqT, gk, gvT, *, tq_pref=256, tk_pref=512):
    B, _, S = gqT.shape
    tq = _pick_tile(S, tq_pref)
    tk = _pick_tile(S, tk_pref)
    kvw = GQA_KV_HEADS * GQA_HEAD_DIM
    hp = max(1, ATTN_COLS // tq)
    assert GQA_GROUP % hp == 0
    n_groups = GQA_HEADS // hp
    return pl.pallas_call(
        functools.partial(_gqa_attn_kernel, tk=tk),
        grid=(B, S // tq),
        in_specs=[pl.BlockSpec((1, GQA_WIDTH, tq), lambda b, i: (b, 0, i)),
                  pl.BlockSpec((1, S, kvw), lambda b, i: (b, 0, 0)),
                  pl.BlockSpec((1, kvw, S), lambda b, i: (b, 0, 0))],
        out_specs=pl.BlockSpec((1, GQA_WIDTH, tq), lambda b, i: (b, 0, i)),
        out_shape=jax.ShapeDtypeStruct((B, GQA_WIDTH, S), BF16),
        scratch_shapes=[pltpu.VMEM((n_groups, kvw, hp * tq), BF16),
                        pltpu.VMEM((2, n_groups, tk, hp * tq), F32),
                        pltpu.VMEM((n_groups, GQA_HEAD_DIM + SUM_ROWS, hp * tq), F32)],
        compiler_params=_params("parallel", "parallel"),
        name="gqa_attn",
    )(gqT, gk, gvT)


def _mix_out_kernel(x_ref, a_ref, b_ref, gates_ref, wa_ref, wb_ref, wo_ref, o_ref):
    D = x_ref.shape[1]
    a = _dot(wa_ref[...], a_ref[0])
    b = _dot(wb_ref[...], b_ref[0])
    merged = gates_ref[0, :D, :].astype(F32) * a + gates_ref[0, D:, :].astype(F32) * b
    o_ref[0] = x_ref[0] + _dot(wo_ref[...], merged.astype(BF16))


def _mix_out(xT, aT, bT, gatesT, waT, wbT, woT, *, ts_pref=512):
    B, D, S = xT.shape
    ts = _pick_tile(S, ts_pref)

    def tr(rows):
        return pl.BlockSpec((1, rows, ts), lambda b, i: (b, 0, i))

    return pl.pallas_call(
        _mix_out_kernel,
        grid=(B, S // ts),
        in_specs=[tr(D), tr(aT.shape[1]), tr(bT.shape[1]), tr(gatesT.shape[1]),
                  _resident(waT.shape), _resident(wbT.shape), _resident(woT.shape)],
        out_specs=tr(D),
        out_shape=jax.ShapeDtypeStruct((B, D, S), F32),
        compiler_params=_params("parallel", "parallel"),
        name="mix_out",
    )(xT, aT, bT, gatesT, waT, wbT, woT)


def _inv_freq(dim):
    return ROPE_THETA ** (-jnp.arange(0, dim, 2, dtype=F32) / dim)


def _rope_tables(S):
    t = jnp.arange(S, dtype=F32)
    row = jnp.floor_divide(jnp.arange(S), GRID_W).astype(F32)
    col = jnp.remainder(jnp.arange(S), GRID_W).astype(F32)
    ang_t = (t[:, None] * _inv_freq(DIFF_HEAD_DIM)[None, :]).T
    ang_r = (row[:, None] * _inv_freq(GQA_AXIS_DIM)[None, :]).T
    ang_c = (col[:, None] * _inv_freq(GQA_AXIS_DIM)[None, :]).T
    cos_d = jnp.concatenate([jnp.cos(ang_t)] * 2, axis=0)
    sin_d = jnp.concatenate([-jnp.sin(ang_t), jnp.sin(ang_t)], axis=0)
    cos_a = jnp.concatenate([jnp.cos(ang_r)] * 2 + [jnp.cos(ang_c)] * 2, axis=0)
    sin_a = jnp.concatenate([-jnp.sin(ang_r), jnp.sin(ang_r), -jnp.sin(ang_c), jnp.sin(ang_c)], axis=0)
    return (cos_d, sin_d), (cos_a, sin_a)


def _gain_table(gain, cos_sin, swap, scale):
    cos, sin = cos_sin
    g = gain.astype(F32)[:, None] * scale
    return jnp.stack([g * cos, swap(g) * sin], axis=0)


def _layer(xT, S, p, lam_init, ropes, first, last):
    (ffn1_norm, ffn1_w_gu, ffn1_w_down, mix_norm, w_in, diff_q_norm, diff_k_norm,
     lambda_q1, lambda_k1, lambda_q2, lambda_k2, diff_subln, gqa_q_norm, gqa_k_norm,
     w_up_diff, w_up_gqa, w_out, ffn2_norm, ffn2_w_gu, ffn2_w_down) = p
    rope_d, rope_a = ropes

    def col(v):
        return v.astype(F32)[:, None]

    def ffn_weights(w_gu, w_down):
        f = w_down.shape[0]
        return (w_gu[:, :f].T.astype(BF16), w_gu[:, f:].T.astype(BF16), w_down.T.astype(BF16))

    xT = _ffn(xT, col(ffn1_norm), *ffn_weights(ffn1_w_gu, ffn1_w_down), in_natural=first)

    tabs = (_gain_table(diff_q_norm, rope_d, _swap_diff, LOG2E / math.sqrt(DIFF_HEAD_DIM)),
            _gain_table(diff_k_norm, rope_d, _swap_diff, 1.0),
            _gain_table(gqa_q_norm, rope_a, _swap_axial, LOG2E / math.sqrt(GQA_HEAD_DIM)),
            _gain_table(gqa_k_norm, rope_a, _swap_axial, 1.0))
    dqT, dk, dvT, gqT, gk, gvT, gatesT = _mix_in(xT, col(mix_norm), w_in.T.astype(BF16), tabs)

    lam_p = jnp.stack([lambda_q1, lambda_k1, lambda_q2, lambda_k2], axis=0).astype(F32)
    aT = _diff_attn(dqT, dk, dvT, lam_p, col(diff_subln), lam_init)
    bT = _gqa_attn(gqT, gk, gvT)
    xT = _mix_out(xT, aT, bT, gatesT, w_up_diff.T.astype(BF16), w_up_gqa.T.astype(BF16),
                  w_out.T.astype(BF16))
    return _ffn(xT, col(ffn2_norm), *ffn_weights(ffn2_w_gu, ffn2_w_down), out_natural=last)


def kernel(x_prompt, x_sample, ffn1_norm, ffn1_w_gu, ffn1_w_down, mix_norm, w_in, diff_q_norm, diff_k_norm, lambda_q1, lambda_k1, lambda_q2, lambda_k2, diff_subln, gqa_q_norm, gqa_k_norm, w_up_diff, w_up_gqa, w_out, ffn2_norm, ffn2_w_gu, ffn2_w_down):
    stacked = (ffn1_norm, ffn1_w_gu, ffn1_w_down, mix_norm, w_in, diff_q_norm, diff_k_norm,
               lambda_q1, lambda_k1, lambda_q2, lambda_k2, diff_subln, gqa_q_norm, gqa_k_norm,
               w_up_diff, w_up_gqa, w_out, ffn2_norm, ffn2_w_gu, ffn2_w_down)
    depth = ffn1_norm.shape[0]
    outs = []
    for x in (x_prompt, x_sample):
        S = x.shape[1]
        ropes = _rope_tables(S)
        y = x
        for l in range(depth):
            lam_init = 0.8 - 0.6 * math.exp(-0.3 * l)
            y = _layer(y, S, tuple(w[l] for w in stacked), lam_init, ropes,
                       first=(l == 0), last=(l == depth - 1))
        outs.append(y)
    return tuple(outs)
```

```python
import functools
import math

import jax
import jax.numpy as jnp
from jax import lax
from jax.experimental import pallas as pl
from jax.experimental.pallas import tpu as pltpu

GRID_W = 64
ROPE_THETA = 10000.0
EPS = 1e-6
DIFF_HEADS = 4
DIFF_HEAD_DIM = 64
DIFF_V_DIM = 2 * DIFF_HEAD_DIM
DIFF_WIDTH = DIFF_HEADS * DIFF_V_DIM
GQA_HEADS = 8
GQA_KV_HEADS = 2
GQA_GROUP = GQA_HEADS // GQA_KV_HEADS
GQA_HEAD_DIM = 64
GQA_AXIS_DIM = GQA_HEAD_DIM // 2
GQA_WIDTH = GQA_HEADS * GQA_HEAD_DIM
HEAD = 64
LOG2E = math.log2(math.e)

SEC_DQ = 0
SEC_DK = SEC_DQ + DIFF_HEADS * 2 * DIFF_HEAD_DIM
SEC_DV = SEC_DK + DIFF_HEADS * 2 * DIFF_HEAD_DIM
SEC_GQ = SEC_DV + DIFF_WIDTH
SEC_GK = SEC_GQ + GQA_WIDTH
SEC_GV = SEC_GK + GQA_KV_HEADS * GQA_HEAD_DIM
SEC_GATES = SEC_GV + GQA_KV_HEADS * GQA_HEAD_DIM

VMEM_LIMIT_BYTES = 56 * 1024 * 1024

BF16 = jnp.bfloat16
F32 = jnp.float32


def _pick_tile(n, pref):
    t = min(n, pref)
    while n % t:
        t //= 2
    return t


def _params(*sem):
    return pltpu.CompilerParams(dimension_semantics=sem, vmem_limit_bytes=VMEM_LIMIT_BYTES)


def _resident(shape):
    nd = len(shape)
    return pl.BlockSpec(shape, lambda *_: (0,) * nd, pipeline_mode=pl.Buffered(1))


def _dot(a, b):
    return jnp.dot(a, b, preferred_element_type=F32)


def _rms_scale(x):
    return lax.rsqrt(jnp.mean(x * x, axis=0, keepdims=True) + EPS)


def _ffn_kernel(x_ref, g_ref, wg_ref, wu_ref, wd_ref, o_ref, *, in_natural, out_natural):
    x = x_ref[0]
    if in_natural:
        x = x.T
    h = (x * _rms_scale(x) * g_ref[...]).astype(BF16)
    gate = _dot(wg_ref[...], h)
    up = _dot(wu_ref[...], h)
    act = (gate * (1.0 / (1.0 + jnp.exp(-gate))) * up).astype(BF16)
    y = x + 0.5 * _dot(wd_ref[...], act)
    o_ref[0] = y.T if out_natural else y


def _ffn(x, g, wgT, wuT, wdT, *, in_natural=False, out_natural=False, ts_pref=512):
    if in_natural:
        B, S, D = x.shape
    else:
        B, D, S = x.shape
    F = wgT.shape[0]
    ts = _pick_tile(S, ts_pref)
    nat = pl.BlockSpec((1, ts, D), lambda b, i: (b, i, 0))
    tr = pl.BlockSpec((1, D, ts), lambda b, i: (b, 0, i))
    out_shape = (B, S, D) if out_natural else (B, D, S)
    return pl.pallas_call(
        functools.partial(_ffn_kernel, in_natural=in_natural, out_natural=out_natural),
        grid=(B, S // ts),
        in_specs=[nat if in_natural else tr, _resident((D, 1)),
                  _resident((F, D)), _resident((F, D)), _resident((D, F))],
        out_specs=nat if out_natural else tr,
        out_shape=jax.ShapeDtypeStruct(out_shape, F32),
        compiler_params=_params("parallel", "parallel"),
        name="ffn",
    )(x, g, wgT, wuT, wdT)


def _swap_diff(x):
    return jnp.concatenate([x[HEAD // 2:], x[:HEAD // 2]], axis=0)


def _swap_axial(x):
    q = HEAD // 4
    return jnp.concatenate([x[q:2 * q], x[:q], x[3 * q:], x[2 * q:3 * q]], axis=0)


def _norm_rope(u, n_groups, tab_ref, swap):
    ta = tab_ref[0]
    tb = tab_ref[1]
    outs = []
    for j in range(n_groups):
        x = u[j * HEAD:(j + 1) * HEAD]
        outs.append(_rms_scale(x) * (x * ta + swap(x) * tb))
    return jnp.concatenate(outs, axis=0)


def _mix_in_kernel(x_ref, g_ref, w_ref, tdq_ref, tdk_ref, tgq_ref, tgk_ref,
                   dq_ref, dk_ref, dv_ref, gq_ref, gk_ref, gv_ref, gates_ref):
    x = x_ref[0]
    h = (x * _rms_scale(x) * g_ref[...]).astype(BF16)

    def proj(lo, hi):
        return _dot(w_ref[lo:hi, :], h)

    n_dqk = DIFF_HEADS * 2
    dq_ref[0] = _norm_rope(proj(SEC_DQ, SEC_DK), n_dqk, tdq_ref, _swap_diff).astype(BF16)
    dk_ref[0] = _norm_rope(proj(SEC_DK, SEC_DV), n_dqk, tdk_ref, _swap_diff).T.astype(BF16)
    dv_ref[0] = proj(SEC_DV, SEC_GQ).astype(BF16)
    gq_ref[0] = _norm_rope(proj(SEC_GQ, SEC_GK), GQA_HEADS, tgq_ref, _swap_axial).astype(BF16)
    gk_ref[0] = _norm_rope(proj(SEC_GK, SEC_GV), GQA_KV_HEADS, tgk_ref, _swap_axial).T.astype(BF16)
    gv_ref[0] = proj(SEC_GV, SEC_GATES).astype(BF16)
    z = proj(SEC_GATES, w_ref.shape[0])
    gates_ref[0] = (1.0 / (1.0 + jnp.exp(-z))).astype(BF16)


def _mix_in(xT, g, w_inT, tabs, *, ts_pref=512):
    B, D, S = xT.shape
    d_in = w_inT.shape[0]
    ts = _pick_tile(S, ts_pref)
    n_gates = d_in - SEC_GATES
    kvw = GQA_KV_HEADS * GQA_HEAD_DIM

    def tr(rows):
        return pl.BlockSpec((1, rows, ts), lambda b, i: (b, 0, i))

    def nat(cols):
        return pl.BlockSpec((1, ts, cols), lambda b, i: (b, i, 0))

    tab = pl.BlockSpec((2, HEAD, ts), lambda b, i: (0, 0, i))
    sds = jax.ShapeDtypeStruct
    return pl.pallas_call(
        _mix_in_kernel,
        grid=(B, S // ts),
        in_specs=[tr(D), _resident((D, 1)), _resident((d_in, D)), tab, tab, tab, tab],
        out_specs=[tr(SEC_DK - SEC_DQ), nat(SEC_DV - SEC_DK), tr(DIFF_WIDTH),
                   tr(GQA_WIDTH), nat(kvw), tr(kvw), tr(n_gates)],
        out_shape=[sds((B, SEC_DK - SEC_DQ, S), BF16), sds((B, S, SEC_DV - SEC_DK), BF16),
                   sds((B, DIFF_WIDTH, S), BF16), sds((B, GQA_WIDTH, S), BF16),
                   sds((B, S, kvw), BF16), sds((B, kvw, S), BF16), sds((B, n_gates, S), BF16)],
        compiler_params=_params("parallel", "parallel"),
        name="mix_in",
    )(xT, g, w_inT, *tabs)


ATTN_COLS = 512
SUM_ROWS = 16
ATTN_TRIP_TILES = 16


def _flash_loop(k_ref, v_ref, qs_ref, s_ref, acc_ref, v_rows, tk):
    n_tiles = k_ref.shape[1] // tk
    assert n_tiles % 2 == 0
    G = qs_ref.shape[0]
    ones_rows = jnp.ones((SUM_ROWS, tk), BF16)

    def key_rows(kt):
        return kt * tk if isinstance(kt, int) else pl.multiple_of(kt * tk, tk)

    def scores(kt, buf, g):
        s = _dot(k_ref[0, pl.ds(key_rows(kt), tk), :], qs_ref[g])
        s_ref[buf, g] = s
        return jnp.max(s, axis=0, keepdims=True)

    def step(kt, buf, ms, tile_max, first=False, prefetch=True):
        vblk = v_ref[0, :, pl.ds(key_rows(kt), tk)]
        new_m, next_max = [], []
        for g in range(G):
            if prefetch:
                next_max.append(scores(kt + 1, 1 - buf, g))
            lo, hi = v_rows[g]
            v_ext = jnp.concatenate([vblk[lo:hi], ones_rows], axis=0)
            if first:
                m_new = tile_max[g]
                acc_ref[g] = _dot(v_ext, jnp.exp2(s_ref[buf, g] - m_new).astype(BF16))
            else:
                m_new = jnp.maximum(ms[g], tile_max[g])
                alpha = jnp.exp2(ms[g] - m_new)
                p = jnp.exp2(s_ref[buf, g] - m_new).astype(BF16)
                acc_ref[g] = alpha * acc_ref[g] + _dot(v_ext, p)
            new_m.append(m_new)
        return tuple(new_m), tuple(next_max)

    def pair(kt, ms, tile_max, first=False, last=False):
        ms, tile_max = step(kt, 0, ms, tile_max, first=first)
        return step(kt + 1, 1, ms, tile_max, prefetch=not last)

    carry = pair(0, None, tuple(scores(0, 0, g) for g in range(G)), first=True, last=n_tiles == 2)
    if n_tiles > 4:
        carry = lax.fori_loop(1, n_tiles // 2 - 1, lambda u, c: pair(2 * u, *c), carry,
                              unroll=max(1, ATTN_TRIP_TILES // (2 * G)))
    if n_tiles > 2:
        pair(n_tiles - 2, *carry, last=True)


def _normalized(acc_ref, g, dv):
    return acc_ref[g, :dv, :] / acc_ref[g, dv:dv + 1, :]


def _diff_attn_kernel(q_ref, k_ref, v_ref, lam_ref, sub_ref, o_ref, qs_ref, s_ref, acc_ref, *,
                      tk, lam_init):
    tq = q_ref.shape[2]
    cols = qs_ref.shape[2]
    nb = tq // cols
    q = q_ref[0]
    first_half = lax.broadcasted_iota(jnp.int32, q.shape, 0) < HEAD
    zero = jnp.zeros_like(q)
    q_maps = (jnp.where(first_half, q, zero), jnp.where(first_half, zero, q))
    for mp in range(2):
        for cblk in range(nb):
            qs_ref[mp * nb + cblk] = q_maps[mp][:, cblk * cols:(cblk + 1) * cols]

    dv = v_ref.shape[1]
    _flash_loop(k_ref, v_ref, qs_ref, s_ref, acc_ref, [(0, dv)] * (2 * nb), tk)

    lam_p = lam_ref[...]
    lam = (jnp.exp(jnp.sum(lam_p[0:1] * lam_p[1:2], axis=1, keepdims=True))
           - jnp.exp(jnp.sum(lam_p[2:3] * lam_p[3:4], axis=1, keepdims=True)) + lam_init)
    for cblk in range(nb):
        d = _normalized(acc_ref, cblk, dv) - lam * _normalized(acc_ref, nb + cblk, dv)
        o_ref[0, :, cblk * cols:(cblk + 1) * cols] = (
            d * _rms_scale(d) * sub_ref[...] * (1.0 - lam_init)).astype(BF16)


def _diff_attn(dqT, dk, dvT, lam_p, sub_g, lam_init, *, tq_pref=1024, tk_pref=512):
    B, _, S = dqT.shape
    tq = _pick_tile(S, tq_pref)
    tk = _pick_tile(S, tk_pref)
    cols = _pick_tile(tq, ATTN_COLS)
    hv = DIFF_V_DIM
    n_chains = 2 * (tq // cols)
    return pl.pallas_call(
        functools.partial(_diff_attn_kernel, tk=tk, lam_init=lam_init),
        grid=(B, DIFF_HEADS, S // tq),
        in_specs=[pl.BlockSpec((1, 2 * DIFF_HEAD_DIM, tq), lambda b, h, i: (b, h, i)),
                  pl.BlockSpec((1, S, 2 * DIFF_HEAD_DIM), lambda b, h, i: (b, 0, h)),
                  pl.BlockSpec((1, hv, S), lambda b, h, i: (b, h, 0)),
                  _resident(lam_p.shape), _resident(sub_g.shape)],
        out_specs=pl.BlockSpec((1, hv, tq), lambda b, h, i: (b, h, i)),
        out_shape=jax.ShapeDtypeStruct((B, DIFF_WIDTH, S), BF16),
        scratch_shapes=[pltpu.VMEM((n_chains, 2 * DIFF_HEAD_DIM, cols), BF16),
                        pltpu.VMEM((2, n_chains, tk, cols), F32),
                        pltpu.VMEM((n_chains, hv + SUM_ROWS, cols), F32)],
        compiler_params=_params("parallel", "parallel", "parallel"),
        name="diff_attn",
    )(dqT, dk, dvT, lam_p, sub_g)


def _gqa_attn_kernel(q_ref, k_ref, v_ref, o_ref, qs_ref, s_ref, acc_ref, *, tk):
    d = GQA_HEAD_DIM
    tq = q_ref.shape[2]
    hp = qs_ref.shape[2] // tq
    zeros = jnp.zeros((d, hp * tq), BF16)
    v_rows = []
    for g in range(GQA_HEADS // hp):
        kv = (g * hp) // GQA_GROUP
        qh = jnp.concatenate([q_ref[0, j * d:(j + 1) * d, :] for j in range(g * hp, (g + 1) * hp)], axis=1)
        qs_ref[g] = jnp.concatenate([qh, zeros] if kv == 0 else [zeros, qh], axis=0)
        v_rows.append((kv * d, (kv + 1) * d))

    _flash_loop(k_ref, v_ref, qs_ref, s_ref, acc_ref, v_rows, tk)
    for g in range(GQA_HEADS // hp):
        o = _normalized(acc_ref, g, d)
        for r in range(hp):
            j = g * hp + r
            o_ref[0, j * d:(j + 1) * d, :] = o[:, r * tq:(r + 1) * tq].astype(BF16)


def _gqa_attn(gqT, gk, gvT, *, tq_pref=512, tk_pref=512):
    B, _, S = gqT.shape
    tq = _pick_tile(S, tq_pref)
    tk = _pick_tile(S, tk_pref)
    kvw = GQA_KV_HEADS * GQA_HEAD_DIM
    hp = max(1, ATTN_COLS // tq)
    assert GQA_GROUP % hp == 0
    n_groups = GQA_HEADS // hp
    return pl.pallas_call(
        functools.partial(_gqa_attn_kernel, tk=tk),
        grid=(B, S // tq),
        in_specs=[pl.BlockSpec((1, GQA_WIDTH, tq), lambda b, i: (b, 0, i)),
                  pl.BlockSpec((1, S, kvw), lambda b, i: (b, 0, 0)),
                  pl.BlockSpec((1, kvw, S), lambda b, i: (b, 0, 0))],
        out_specs=pl.BlockSpec((1, GQA_WIDTH, tq), lambda b, i: (b, 0, i)),
        out_shape=jax.ShapeDtypeStruct((B, GQA_WIDTH, S), BF16),
        scratch_shapes=[pltpu.VMEM((n_groups, kvw, hp * tq), BF16),
                        pltpu.VMEM((2, n_groups, tk, hp * tq), F32),
                        pltpu.VMEM((n_groups, GQA_HEAD_DIM + SUM_ROWS, hp * tq), F32)],
        compiler_params=_params("parallel", "parallel"),
        name="gqa_attn",
    )(gqT, gk, gvT)


def _mix_out_kernel(x_ref, a_ref, b_ref, gates_ref, wa_ref, wb_ref, wo_ref, o_ref):
    D = x_ref.shape[1]
    a = _dot(wa_ref[...], a_ref[0])
    b = _dot(wb_ref[...], b_ref[0])
    merged = gates_ref[0, :D, :].astype(F32) * a + gates_ref[0, D:, :].astype(F32) * b
    o_ref[0] = x_ref[0] + _dot(wo_ref[...], merged.astype(BF16))


def _mix_out(xT, aT, bT, gatesT, waT, wbT, woT, *, ts_pref=512):
    B, D, S = xT.shape
    ts = _pick_tile(S, ts_pref)

    def tr(rows):
        return pl.BlockSpec((1, rows, ts), lambda b, i: (b, 0, i))

    return pl.pallas_call(
        _mix_out_kernel,
        grid=(B, S // ts),
        in_specs=[tr(D), tr(aT.shape[1]), tr(bT.shape[1]), tr(gatesT.shape[1]),
                  _resident(waT.shape), _resident(wbT.shape), _resident(woT.shape)],
        out_specs=tr(D),
        out_shape=jax.ShapeDtypeStruct((B, D, S), F32),
        compiler_params=_params("parallel", "parallel"),
        name="mix_out",
    )(xT, aT, bT, gatesT, waT, wbT, woT)


def _inv_freq(dim):
    return ROPE_THETA ** (-jnp.arange(0, dim, 2, dtype=F32) / dim)


def _rope_tables(S):
    t = jnp.arange(S, dtype=F32)
    row = jnp.floor_divide(jnp.arange(S), GRID_W).astype(F32)
    col = jnp.remainder(jnp.arange(S), GRID_W).astype(F32)
    ang_t = (t[:, None] * _inv_freq(DIFF_HEAD_DIM)[None, :]).T
    ang_r = (row[:, None] * _inv_freq(GQA_AXIS_DIM)[None, :]).T
    ang_c = (col[:, None] * _inv_freq(GQA_AXIS_DIM)[None, :]).T
    cos_d = jnp.concatenate([jnp.cos(ang_t)] * 2, axis=0)
    sin_d = jnp.concatenate([-jnp.sin(ang_t), jnp.sin(ang_t)], axis=0)
    cos_a = jnp.concatenate([jnp.cos(ang_r)] * 2 + [jnp.cos(ang_c)] * 2, axis=0)
    sin_a = jnp.concatenate([-jnp.sin(ang_r), jnp.sin(ang_r), -jnp.sin(ang_c), jnp.sin(ang_c)], axis=0)
    return (cos_d, sin_d), (cos_a, sin_a)


def _gain_table(gain, cos_sin, swap, scale):
    cos, sin = cos_sin
    g = gain.astype(F32)[:, None] * scale
    return jnp.stack([g * cos, swap(g) * sin], axis=0)


def _layer(xT, S, p, lam_init, ropes, first, last):
    (ffn1_norm, ffn1_w_gu, ffn1_w_down, mix_norm, w_in, diff_q_norm, diff_k_norm,
     lambda_q1, lambda_k1, lambda_q2, lambda_k2, diff_subln, gqa_q_norm, gqa_k_norm,
     w_up_diff, w_up_gqa, w_out, ffn2_norm, ffn2_w_gu, ffn2_w_down) = p
    rope_d, rope_a = ropes

    def col(v):
        return v.astype(F32)[:, None]

    def ffn_weights(w_gu, w_down):
        f = w_down.shape[0]
        return (w_gu[:, :f].T.astype(BF16), w_gu[:, f:].T.astype(BF16), w_down.T.astype(BF16))

    xT = _ffn(xT, col(ffn1_norm), *ffn_weights(ffn1_w_gu, ffn1_w_down), in_natural=first)

    tabs = (_gain_table(diff_q_norm, rope_d, _swap_diff, LOG2E / math.sqrt(DIFF_HEAD_DIM)),
            _gain_table(diff_k_norm, rope_d, _swap_diff, 1.0),
            _gain_table(gqa_q_norm, rope_a, _swap_axial, LOG2E / math.sqrt(GQA_HEAD_DIM)),
            _gain_table(gqa_k_norm, rope_a, _swap_axial, 1.0))
    dqT, dk, dvT, gqT, gk, gvT, gatesT = _mix_in(xT, col(mix_norm), w_in.T.astype(BF16), tabs)

    lam_p = jnp.stack([lambda_q1, lambda_k1, lambda_q2, lambda_k2], axis=0).astype(F32)
    aT = _diff_attn(dqT, dk, dvT, lam_p, col(diff_subln), lam_init)
    bT = _gqa_attn(gqT, gk, gvT)
    xT = _mix_out(xT, aT, bT, gatesT, w_up_diff.T.astype(BF16), w_up_gqa.T.astype(BF16),
                  w_out.T.astype(BF16))
    return _ffn(xT, col(ffn2_norm), *ffn_weights(ffn2_w_gu, ffn2_w_down), out_natural=last)


def kernel(x_prompt, x_sample, ffn1_norm, ffn1_w_gu, ffn1_w_down, mix_norm, w_in, diff_q_norm, diff_k_norm, lambda_q1, lambda_k1, lambda_q2, lambda_k2, diff_subln, gqa_q_norm, gqa_k_norm, w_up_diff, w_up_gqa, w_out, ffn2_norm, ffn2_w_gu, ffn2_w_down):
    stacked = (ffn1_norm, ffn1_w_gu, ffn1_w_down, mix_norm, w_in, diff_q_norm, diff_k_norm,
               lambda_q1, lambda_k1, lambda_q2, lambda_k2, diff_subln, gqa_q_norm, gqa_k_norm,
               w_up_diff, w_up_gqa, w_out, ffn2_norm, ffn2_w_gu, ffn2_w_down)
    depth = ffn1_norm.shape[0]
    outs = []
    for x in (x_prompt, x_sample):
        S = x.shape[1]
        ropes = _rope_tables(S)
        y = x
        for l in range(depth):
            lam_init = 0.8 - 0.6 * math.exp(-0.3 * l)
            y = _layer(y, S, tuple(w[l] for w in stacked), lam_init, ropes,
                       first=(l == 0), last=(l == depth - 1))
        outs.append(y)
    return tuple(outs)
```

```python
import functools
import math

import jax
import jax.numpy as jnp
from jax import lax
from jax.experimental import pallas as pl
from jax.experimental.pallas import tpu as pltpu

GRID_W = 64
ROPE_THETA = 10000.0
EPS = 1e-6
DIFF_HEADS = 4
DIFF_HEAD_DIM = 64
DIFF_V_DIM = 2 * DIFF_HEAD_DIM
DIFF_WIDTH = DIFF_HEADS * DIFF_V_DIM
GQA_HEADS = 8
GQA_KV_HEADS = 2
GQA_GROUP = GQA_HEADS // GQA_KV_HEADS
GQA_HEAD_DIM = 64
GQA_AXIS_DIM = GQA_HEAD_DIM // 2
GQA_WIDTH = GQA_HEADS * GQA_HEAD_DIM
HEAD = 64
LOG2E = math.log2(math.e)

SEC_DQ = 0
SEC_DK = SEC_DQ + DIFF_HEADS * 2 * DIFF_HEAD_DIM
SEC_DV = SEC_DK + DIFF_HEADS * 2 * DIFF_HEAD_DIM
SEC_GQ = SEC_DV + DIFF_WIDTH
SEC_GK = SEC_GQ + GQA_WIDTH
SEC_GV = SEC_GK + GQA_KV_HEADS * GQA_HEAD_DIM
SEC_GATES = SEC_GV + GQA_KV_HEADS * GQA_HEAD_DIM

VMEM_LIMIT_BYTES = 56 * 1024 * 1024

BF16 = jnp.bfloat16
F32 = jnp.float32


def _pick_tile(n, pref):
    t = min(n, pref)
    while n % t:
        t //= 2
    return t


def _params(*sem):
    return pltpu.CompilerParams(dimension_semantics=sem, vmem_limit_bytes=VMEM_LIMIT_BYTES)


def _resident(shape):
    nd = len(shape)
    return pl.BlockSpec(shape, lambda *_: (0,) * nd, pipeline_mode=pl.Buffered(1))


def _dot(a, b):
    return jnp.dot(a, b, preferred_element_type=F32)


def _rms_scale(x):
    return lax.rsqrt(jnp.mean(x * x, axis=0, keepdims=True) + EPS)


def _ffn_kernel(x_ref, g_ref, wg_ref, wu_ref, wd_ref, o_ref, *, in_natural, out_natural):
    x = x_ref[0]
    if in_natural:
        x = x.T
    h = (x * _rms_scale(x) * g_ref[...]).astype(BF16)
    gate = _dot(wg_ref[...], h)
    up = _dot(wu_ref[...], h)
    act = (gate * (1.0 / (1.0 + jnp.exp(-gate))) * up).astype(BF16)
    y = x + 0.5 * _dot(wd_ref[...], act)
    o_ref[0] = y.T if out_natural else y


def _ffn(x, g, wgT, wuT, wdT, *, in_natural=False, out_natural=False, ts_pref=512):
    if in_natural:
        B, S, D = x.shape
    else:
        B, D, S = x.shape
    F = wgT.shape[0]
    ts = _pick_tile(S, ts_pref)
    nat = pl.BlockSpec((1, ts, D), lambda b, i: (b, i, 0))
    tr = pl.BlockSpec((1, D, ts), lambda b, i: (b, 0, i))
    out_shape = (B, S, D) if out_natural else (B, D, S)
    return pl.pallas_call(
        functools.partial(_ffn_kernel, in_natural=in_natural, out_natural=out_natural),
        grid=(B, S // ts),
        in_specs=[nat if in_natural else tr, _resident((D, 1)),
                  _resident((F, D)), _resident((F, D)), _resident((D, F))],
        out_specs=nat if out_natural else tr,
        out_shape=jax.ShapeDtypeStruct(out_shape, F32),
        compiler_params=_params("parallel", "parallel"),
        name="ffn",
    )(x, g, wgT, wuT, wdT)


def _swap_diff(x):
    return jnp.concatenate([x[HEAD // 2:], x[:HEAD // 2]], axis=0)


def _swap_axial(x):
    q = HEAD // 4
    return jnp.concatenate([x[q:2 * q], x[:q], x[3 * q:], x[2 * q:3 * q]], axis=0)


def _norm_rope(u, n_groups, tab_ref, swap):
    ta = tab_ref[0]
    tb = tab_ref[1]
    outs = []
    for j in range(n_groups):
        x = u[j * HEAD:(j + 1) * HEAD]
        outs.append(_rms_scale(x) * (x * ta + swap(x) * tb))
    return jnp.concatenate(outs, axis=0)


def _mix_in_kernel(x_ref, g_ref, w_ref, tdq_ref, tdk_ref, tgq_ref, tgk_ref,
                   dq_ref, dk_ref, dv_ref, gq_ref, gk_ref, gv_ref, gates_ref):
    x = x_ref[0]
    h = (x * _rms_scale(x) * g_ref[...]).astype(BF16)

    def proj(lo, hi):
        return _dot(w_ref[lo:hi, :], h)

    n_dqk = DIFF_HEADS * 2
    dq_ref[0] = _norm_rope(proj(SEC_DQ, SEC_DK), n_dqk, tdq_ref, _swap_diff).astype(BF16)
    dk_ref[0] = _norm_rope(proj(SEC_DK, SEC_DV), n_dqk, tdk_ref, _swap_diff).T.astype(BF16)
    dv_ref[0] = proj(SEC_DV, SEC_GQ).astype(BF16)
    gq_ref[0] = _norm_rope(proj(SEC_GQ, SEC_GK), GQA_HEADS, tgq_ref, _swap_axial).astype(BF16)
    gk_ref[0] = _norm_rope(proj(SEC_GK, SEC_GV), GQA_KV_HEADS, tgk_ref, _swap_axial).T.astype(BF16)
    gv_ref[0] = proj(SEC_GV, SEC_GATES).astype(BF16)
    z = proj(SEC_GATES, w_ref.shape[0])
    gates_ref[0] = (1.0 / (1.0 + jnp.exp(-z))).astype(BF16)


def _mix_in(xT, g, w_inT, tabs, *, ts_pref=512):
    B, D, S = xT.shape
    d_in = w_inT.shape[0]
    ts = _pick_tile(S, ts_pref)
    n_gates = d_in - SEC_GATES
    kvw = GQA_KV_HEADS * GQA_HEAD_DIM

    def tr(rows):
        return pl.BlockSpec((1, rows, ts), lambda b, i: (b, 0, i))

    def nat(cols):
        return pl.BlockSpec((1, ts, cols), lambda b, i: (b, i, 0))

    tab = pl.BlockSpec((2, HEAD, ts), lambda b, i: (0, 0, i))
    sds = jax.ShapeDtypeStruct
    return pl.pallas_call(
        _mix_in_kernel,
        grid=(B, S // ts),
        in_specs=[tr(D), _resident((D, 1)), _resident((d_in, D)), tab, tab, tab, tab],
        out_specs=[tr(SEC_DK - SEC_DQ), nat(SEC_DV - SEC_DK), tr(DIFF_WIDTH),
                   tr(GQA_WIDTH), nat(kvw), tr(kvw), tr(n_gates)],
        out_shape=[sds((B, SEC_DK - SEC_DQ, S), BF16), sds((B, S, SEC_DV - SEC_DK), BF16),
                   sds((B, DIFF_WIDTH, S), BF16), sds((B, GQA_WIDTH, S), BF16),
                   sds((B, S, kvw), BF16), sds((B, kvw, S), BF16), sds((B, n_gates, S), BF16)],
        compiler_params=_params("parallel", "parallel"),
        name="mix_in",
    )(xT, g, w_inT, *tabs)


ATTN_COLS = 512
SUM_ROWS = 16
ATTN_TRIP_TILES = 16
EXP2_SAFE_RANGE = 64.0


def _fold_rows(x):
    slabs = [x[r:r + 8] for r in range(0, x.shape[0], 8)]
    while len(slabs) > 1:
        slabs = [a + b for a, b in zip(slabs[::2], slabs[1::2])]
    return slabs[0]


def _key_loop(k_ref, v_ref, qs_ref, stage_ref, acc_ref, v_rows, tk, bounded):
    n_tiles = k_ref.shape[1] // tk
    assert n_tiles % 2 == 0
    G = qs_ref.shape[0]
    dv = acc_ref.shape[1] - (0 if bounded else SUM_ROWS)
    ones_rows = jnp.ones((SUM_ROWS, tk), BF16)

    def key_rows(kt):
        return kt * tk if isinstance(kt, int) else pl.multiple_of(kt * tk, tk)

    def produce(kt, buf, g):
        s = _dot(k_ref[0, pl.ds(key_rows(kt), tk), :], qs_ref[g])
        if bounded:
            p = jnp.exp2(s)
            stage_ref[buf, g] = p.astype(BF16)
            return _fold_rows(p)
        stage_ref[buf, g] = s
        return jnp.max(s, axis=0, keepdims=True)

    def step(kt, buf, state, stat, first=False, prefetch=True):
        vblk = v_ref[0, :, pl.ds(key_rows(kt), tk)]
        new_state, next_stat = [], []
        for g in range(G):
            if prefetch:
                next_stat.append(produce(kt + 1, 1 - buf, g))
            lo, hi = v_rows[g]
            if bounded:
                pv = _dot(vblk[lo:hi], stage_ref[buf, g])
                acc_ref[g] = pv if first else acc_ref[g] + pv
                new_state.append(stat[g] if first else state[g] + stat[g])
                continue
            v_ext = jnp.concatenate([vblk[lo:hi], ones_rows], axis=0)
            if first:
                m_new = stat[g]
                acc_ref[g] = _dot(v_ext, jnp.exp2(stage_ref[buf, g] - m_new).astype(BF16))
            else:
                m_new = jnp.maximum(state[g], stat[g])
                alpha = jnp.exp2(state[g] - m_new)
                p = jnp.exp2(stage_ref[buf, g] - m_new).astype(BF16)
                acc_ref[g] = alpha * acc_ref[g] + _dot(v_ext, p)
            new_state.append(m_new)
        return tuple(new_state), tuple(next_stat)

    def pair(kt, state, stat, first=False, last=False):
        state, stat = step(kt, 0, state, stat, first=first)
        return step(kt + 1, 1, state, stat, prefetch=not last)

    carry = pair(0, None, tuple(produce(0, 0, g) for g in range(G)), first=True, last=n_tiles == 2)
    if n_tiles > 4:
        carry = lax.fori_loop(1, n_tiles // 2 - 1, lambda u, c: pair(2 * u, *c), carry,
                              unroll=max(1, ATTN_TRIP_TILES // (2 * G)))
    if n_tiles > 2:
        carry = pair(n_tiles - 2, *carry, last=True)
    if bounded:
        return [jnp.sum(part, axis=0, keepdims=True) for part in carry[0]]
    return [acc_ref[g, dv:dv + 1, :] for g in range(G)]


def _attn_scratch(n_groups, q_rows, dv, tk, cols, bounded):
    return [pltpu.VMEM((n_groups, q_rows, cols), BF16),
            pltpu.VMEM((2, n_groups, tk, cols), BF16 if bounded else F32),
            pltpu.VMEM((n_groups, dv + (0 if bounded else SUM_ROWS), cols), F32)]


def _diff_attn_kernel(q_ref, k_ref, v_ref, lam_ref, sub_ref, o_ref, qs_ref, stage_ref, acc_ref, *,
                      tk, lam_init, bounded):
    tq = q_ref.shape[2]
    cols = qs_ref.shape[2]
    nb = tq // cols
    q = q_ref[0]
    first_half = lax.broadcasted_iota(jnp.int32, q.shape, 0) < HEAD
    zero = jnp.zeros_like(q)
    q_maps = (jnp.where(first_half, q, zero), jnp.where(first_half, zero, q))
    for mp in range(2):
        for cblk in range(nb):
            qs_ref[mp * nb + cblk] = q_maps[mp][:, cblk * cols:(cblk + 1) * cols]

    dv = v_ref.shape[1]
    ls = _key_loop(k_ref, v_ref, qs_ref, stage_ref, acc_ref, [(0, dv)] * (2 * nb), tk, bounded)

    lam_p = lam_ref[...]
    lam = (jnp.exp(jnp.sum(lam_p[0:1] * lam_p[1:2], axis=1, keepdims=True))
           - jnp.exp(jnp.sum(lam_p[2:3] * lam_p[3:4], axis=1, keepdims=True)) + lam_init)
    for cblk in range(nb):
        d = (acc_ref[cblk, :dv, :] / ls[cblk]
             - lam * (acc_ref[nb + cblk, :dv, :] / ls[nb + cblk]))
        o_ref[0, :, cblk * cols:(cblk + 1) * cols] = (
            d * _rms_scale(d) * sub_ref[...] * (1.0 - lam_init)).astype(BF16)


def _diff_attn(dqT, dk, dvT, lam_p, sub_g, *, lam_init, bounded, tq_pref=1024, tk_pref=512):
    B, _, S = dqT.shape
    tq = _pick_tile(S, tq_pref)
    tk = _pick_tile(S, tk_pref)
    cols = _pick_tile(tq, ATTN_COLS)
    hv = DIFF_V_DIM
    return pl.pallas_call(
        functools.partial(_diff_attn_kernel, tk=tk, lam_init=lam_init, bounded=bounded),
        grid=(B, DIFF_HEADS, S // tq),
        in_specs=[pl.BlockSpec((1, 2 * DIFF_HEAD_DIM, tq), lambda b, h, i: (b, h, i)),
                  pl.BlockSpec((1, S, 2 * DIFF_HEAD_DIM), lambda b, h, i: (b, 0, h)),
                  pl.BlockSpec((1, hv, S), lambda b, h, i: (b, h, 0)),
                  _resident(lam_p.shape), _resident(sub_g.shape)],
        out_specs=pl.BlockSpec((1, hv, tq), lambda b, h, i: (b, h, i)),
        out_shape=jax.ShapeDtypeStruct((B, DIFF_WIDTH, S), BF16),
        scratch_shapes=_attn_scratch(2 * (tq // cols), 2 * DIFF_HEAD_DIM, hv, tk, cols, bounded),
        compiler_params=_params("parallel", "parallel", "parallel"),
        name="diff_attn_bounded" if bounded else "diff_attn",
    )(dqT, dk, dvT, lam_p, sub_g)


def _gqa_attn_kernel(q_ref, k_ref, v_ref, o_ref, qs_ref, stage_ref, acc_ref, *, tk, bounded):
    d = GQA_HEAD_DIM
    tq = q_ref.shape[2]
    hp = qs_ref.shape[2] // tq
    zeros = jnp.zeros((d, hp * tq), BF16)
    v_rows = []
    for g in range(GQA_HEADS // hp):
        kv = (g * hp) // GQA_GROUP
        qh = jnp.concatenate([q_ref[0, j * d:(j + 1) * d, :] for j in range(g * hp, (g + 1) * hp)], axis=1)
        qs_ref[g] = jnp.concatenate([qh, zeros] if kv == 0 else [zeros, qh], axis=0)
        v_rows.append((kv * d, (kv + 1) * d))

    ls = _key_loop(k_ref, v_ref, qs_ref, stage_ref, acc_ref, v_rows, tk, bounded)
    for g in range(GQA_HEADS // hp):
        o = acc_ref[g, :d, :] / ls[g]
        for r in range(hp):
            j = g * hp + r
            o_ref[0, j * d:(j + 1) * d, :] = o[:, r * tq:(r + 1) * tq].astype(BF16)


def _gqa_attn(gqT, gk, gvT, *, bounded, tq_pref=512, tk_pref=512):
    B, _, S = gqT.shape
    tq = _pick_tile(S, tq_pref)
    tk = _pick_tile(S, tk_pref)
    kvw = GQA_KV_HEADS * GQA_HEAD_DIM
    hp = max(1, ATTN_COLS // tq)
    assert GQA_GROUP % hp == 0
    return pl.pallas_call(
        functools.partial(_gqa_attn_kernel, tk=tk, bounded=bounded),
        grid=(B, S // tq),
        in_specs=[pl.BlockSpec((1, GQA_WIDTH, tq), lambda b, i: (b, 0, i)),
                  pl.BlockSpec((1, S, kvw), lambda b, i: (b, 0, 0)),
                  pl.BlockSpec((1, kvw, S), lambda b, i: (b, 0, 0))],
        out_specs=pl.BlockSpec((1, GQA_WIDTH, tq), lambda b, i: (b, 0, i)),
        out_shape=jax.ShapeDtypeStruct((B, GQA_WIDTH, S), BF16),
        scratch_shapes=_attn_scratch(GQA_HEADS // hp, kvw, GQA_HEAD_DIM, tk, hp * tq, bounded),
        compiler_params=_params("parallel", "parallel"),
        name="gqa_attn_bounded" if bounded else "gqa_attn",
    )(gqT, gk, gvT)


def _score_bound(q_gain, k_gain, q_scale):
    return 1.02 * q_scale * HEAD * jnp.max(jnp.abs(q_gain)) * jnp.max(jnp.abs(k_gain))


def _mix_out_kernel(x_ref, a_ref, b_ref, gates_ref, wa_ref, wb_ref, wo_ref, o_ref):
    D = x_ref.shape[1]
    a = _dot(wa_ref[...], a_ref[0])
    b = _dot(wb_ref[...], b_ref[0])
    merged = gates_ref[0, :D, :].astype(F32) * a + gates_ref[0, D:, :].astype(F32) * b
    o_ref[0] = x_ref[0] + _dot(wo_ref[...], merged.astype(BF16))


def _mix_out(xT, aT, bT, gatesT, waT, wbT, woT, *, ts_pref=512):
    B, D, S = xT.shape
    ts = _pick_tile(S, ts_pref)

    def tr(rows):
        return pl.BlockSpec((1, rows, ts), lambda b, i: (b, 0, i))

    return pl.pallas_call(
        _mix_out_kernel,
        grid=(B, S // ts),
        in_specs=[tr(D), tr(aT.shape[1]), tr(bT.shape[1]), tr(gatesT.shape[1]),
                  _resident(waT.shape), _resident(wbT.shape), _resident(woT.shape)],
        out_specs=tr(D),
        out_shape=jax.ShapeDtypeStruct((B, D, S), F32),
        compiler_params=_params("parallel", "parallel"),
        name="mix_out",
    )(xT, aT, bT, gatesT, waT, wbT, woT)


def _inv_freq(dim):
    return ROPE_THETA ** (-jnp.arange(0, dim, 2, dtype=F32) / dim)


def _rope_tables(S):
    t = jnp.arange(S, dtype=F32)
    row = jnp.floor_divide(jnp.arange(S), GRID_W).astype(F32)
    col = jnp.remainder(jnp.arange(S), GRID_W).astype(F32)
    ang_t = (t[:, None] * _inv_freq(DIFF_HEAD_DIM)[None, :]).T
    ang_r = (row[:, None] * _inv_freq(GQA_AXIS_DIM)[None, :]).T
    ang_c = (col[:, None] * _inv_freq(GQA_AXIS_DIM)[None, :]).T
    cos_d = jnp.concatenate([jnp.cos(ang_t)] * 2, axis=0)
    sin_d = jnp.concatenate([-jnp.sin(ang_t), jnp.sin(ang_t)], axis=0)
    cos_a = jnp.concatenate([jnp.cos(ang_r)] * 2 + [jnp.cos(ang_c)] * 2, axis=0)
    sin_a = jnp.concatenate([-jnp.sin(ang_r), jnp.sin(ang_r), -jnp.sin(ang_c), jnp.sin(ang_c)], axis=0)
    return (cos_d, sin_d), (cos_a, sin_a)


def _gain_table(gain, cos_sin, swap, scale):
    cos, sin = cos_sin
    g = gain.astype(F32)[:, None] * scale
    return jnp.stack([g * cos, swap(g) * sin], axis=0)


def _layer(xT, S, p, lam_init, ropes, first, last):
    (ffn1_norm, ffn1_w_gu, ffn1_w_down, mix_norm, w_in, diff_q_norm, diff_k_norm,
     lambda_q1, lambda_k1, lambda_q2, lambda_k2, diff_subln, gqa_q_norm, gqa_k_norm,
     w_up_diff, w_up_gqa, w_out, ffn2_norm, ffn2_w_gu, ffn2_w_down) = p
    rope_d, rope_a = ropes

    def col(v):
        return v.astype(F32)[:, None]

    def ffn_weights(w_gu, w_down):
        f = w_down.shape[0]
        return (w_gu[:, :f].T.astype(BF16), w_gu[:, f:].T.astype(BF16), w_down.T.astype(BF16))

    xT = _ffn(xT, col(ffn1_norm), *ffn_weights(ffn1_w_gu, ffn1_w_down), in_natural=first)

    diff_scale = LOG2E / math.sqrt(DIFF_HEAD_DIM)
    gqa_scale = LOG2E / math.sqrt(GQA_HEAD_DIM)
    tabs = (_gain_table(diff_q_norm, rope_d, _swap_diff, diff_scale),
            _gain_table(diff_k_norm, rope_d, _swap_diff, 1.0),
            _gain_table(gqa_q_norm, rope_a, _swap_axial, gqa_scale),
            _gain_table(gqa_k_norm, rope_a, _swap_axial, 1.0))
    dqT, dk, dvT, gqT, gk, gvT, gatesT = _mix_in(xT, col(mix_norm), w_in.T.astype(BF16), tabs)

    lam_p = jnp.stack([lambda_q1, lambda_k1, lambda_q2, lambda_k2], axis=0).astype(F32)
    aT = lax.cond(_score_bound(diff_q_norm, diff_k_norm, diff_scale) <= EXP2_SAFE_RANGE,
                  functools.partial(_diff_attn, lam_init=lam_init, bounded=True),
                  functools.partial(_diff_attn, lam_init=lam_init, bounded=False),
                  dqT, dk, dvT, lam_p, col(diff_subln))
    bT = lax.cond(_score_bound(gqa_q_norm, gqa_k_norm, gqa_scale) <= EXP2_SAFE_RANGE,
                  functools.partial(_gqa_attn, bounded=True),
                  functools.partial(_gqa_attn, bounded=False),
                  gqT, gk, gvT)
    xT = _mix_out(xT, aT, bT, gatesT, w_up_diff.T.astype(BF16), w_up_gqa.T.astype(BF16),
                  w_out.T.astype(BF16))
    return _ffn(xT, col(ffn2_norm), *ffn_weights(ffn2_w_gu, ffn2_w_down), out_natural=last)


def kernel(x_prompt, x_sample, ffn1_norm, ffn1_w_gu, ffn1_w_down, mix_norm, w_in, diff_q_norm, diff_k_norm, lambda_q1, lambda_k1, lambda_q2, lambda_k2, diff_subln, gqa_q_norm, gqa_k_norm, w_up_diff, w_up_gqa, w_out, ffn2_norm, ffn2_w_gu, ffn2_w_down):
    stacked = (ffn1_norm, ffn1_w_gu, ffn1_w_down, mix_norm, w_in, diff_q_norm, diff_k_norm,
               lambda_q1, lambda_k1, lambda_q2, lambda_k2, diff_subln, gqa_q_norm, gqa_k_norm,
               w_up_diff, w_up_gqa, w_out, ffn2_norm, ffn2_w_gu, ffn2_w_down)
    depth = ffn1_norm.shape[0]
    outs = []
    for x in (x_prompt, x_sample):
        S = x.shape[1]
        ropes = _rope_tables(S)
        y = x
        for l in range(depth):
            lam_init = 0.8 - 0.6 * math.exp(-0.3 * l)
            y = _layer(y, S, tuple(w[l] for w in stacked), lam_init, ropes,
                       first=(l == 0), last=(l == depth - 1))
        outs.append(y)
    return tuple(outs)
```

```python
import functools
import math

import jax
import jax.numpy as jnp
from jax import lax
from jax.experimental import pallas as pl
from jax.experimental.pallas import tpu as pltpu

GRID_W = 64
ROPE_THETA = 10000.0
EPS = 1e-6
DIFF_HEADS = 4
DIFF_HEAD_DIM = 64
DIFF_V_DIM = 2 * DIFF_HEAD_DIM
DIFF_WIDTH = DIFF_HEADS * DIFF_V_DIM
GQA_HEADS = 8
GQA_KV_HEADS = 2
GQA_GROUP = GQA_HEADS // GQA_KV_HEADS
GQA_HEAD_DIM = 64
GQA_AXIS_DIM = GQA_HEAD_DIM // 2
GQA_WIDTH = GQA_HEADS * GQA_HEAD_DIM
HEAD = 64
LOG2E = math.log2(math.e)

SEC_DQ = 0
SEC_DK = SEC_DQ + DIFF_HEADS * 2 * DIFF_HEAD_DIM
SEC_DV = SEC_DK + DIFF_HEADS * 2 * DIFF_HEAD_DIM
SEC_GQ = SEC_DV + DIFF_WIDTH
SEC_GK = SEC_GQ + GQA_WIDTH
SEC_GV = SEC_GK + GQA_KV_HEADS * GQA_HEAD_DIM
SEC_GATES = SEC_GV + GQA_KV_HEADS * GQA_HEAD_DIM

VMEM_LIMIT_BYTES = 56 * 1024 * 1024

BF16 = jnp.bfloat16
F32 = jnp.float32


def _pick_tile(n, pref):
    t = min(n, pref)
    while n % t:
        t //= 2
    return t


def _params(*sem):
    return pltpu.CompilerParams(dimension_semantics=sem, vmem_limit_bytes=VMEM_LIMIT_BYTES)


def _resident(shape):
    nd = len(shape)
    return pl.BlockSpec(shape, lambda *_: (0,) * nd, pipeline_mode=pl.Buffered(1))


def _dot(a, b):
    return jnp.dot(a, b, preferred_element_type=F32)


def _rms_scale(x):
    return lax.rsqrt(jnp.mean(x * x, axis=0, keepdims=True) + EPS)


def _ffn_kernel(x_ref, g_ref, wg_ref, wu_ref, wd_ref, o_ref, *, in_natural, out_natural):
    x = x_ref[0]
    if in_natural:
        x = x.T
    h = (x * _rms_scale(x) * g_ref[...]).astype(BF16)
    gate = _dot(wg_ref[...], h)
    up = _dot(wu_ref[...], h)
    act = (gate * (1.0 / (1.0 + jnp.exp(-gate))) * up).astype(BF16)
    y = x + 0.5 * _dot(wd_ref[...], act)
    o_ref[0] = y.T if out_natural else y


def _ffn(x, g, wgT, wuT, wdT, *, in_natural=False, out_natural=False, ts_pref=512):
    if in_natural:
        B, S, D = x.shape
    else:
        B, D, S = x.shape
    F = wgT.shape[0]
    ts = _pick_tile(S, ts_pref)
    nat = pl.BlockSpec((1, ts, D), lambda b, i: (b, i, 0))
    tr = pl.BlockSpec((1, D, ts), lambda b, i: (b, 0, i))
    out_shape = (B, S, D) if out_natural else (B, D, S)
    return pl.pallas_call(
        functools.partial(_ffn_kernel, in_natural=in_natural, out_natural=out_natural),
        grid=(B, S // ts),
        in_specs=[nat if in_natural else tr, _resident((D, 1)),
                  _resident((F, D)), _resident((F, D)), _resident((D, F))],
        out_specs=nat if out_natural else tr,
        out_shape=jax.ShapeDtypeStruct(out_shape, F32),
        compiler_params=_params("parallel", "parallel"),
        name="ffn",
    )(x, g, wgT, wuT, wdT)


def _swap_diff(x):
    return jnp.concatenate([x[HEAD // 2:], x[:HEAD // 2]], axis=0)


def _swap_axial(x):
    q = HEAD // 4
    return jnp.concatenate([x[q:2 * q], x[:q], x[3 * q:], x[2 * q:3 * q]], axis=0)


def _norm_rope(u, n_groups, tab_ref, swap):
    ta = tab_ref[0]
    tb = tab_ref[1]
    outs = []
    for j in range(n_groups):
        x = u[j * HEAD:(j + 1) * HEAD]
        outs.append(_rms_scale(x) * (x * ta + swap(x) * tb))
    return jnp.concatenate(outs, axis=0)


def _mix_in_kernel(x_ref, g_ref, w_ref, tdq_ref, tdk_ref, tgq_ref, tgk_ref,
                   dq_ref, dk_ref, dv_ref, gq_ref, gk_ref, gv_ref, gates_ref):
    x = x_ref[0]
    h = (x * _rms_scale(x) * g_ref[...]).astype(BF16)

    def proj(lo, hi):
        return _dot(w_ref[lo:hi, :], h)

    n_dqk = DIFF_HEADS * 2
    dq_ref[0] = _norm_rope(proj(SEC_DQ, SEC_DK), n_dqk, tdq_ref, _swap_diff).astype(BF16)
    dk_ref[0] = _norm_rope(proj(SEC_DK, SEC_DV), n_dqk, tdk_ref, _swap_diff).T.astype(BF16)
    dv_ref[0] = proj(SEC_DV, SEC_GQ).astype(BF16)
    gq_ref[0] = _norm_rope(proj(SEC_GQ, SEC_GK), GQA_HEADS, tgq_ref, _swap_axial).astype(BF16)
    gk_ref[0] = _norm_rope(proj(SEC_GK, SEC_GV), GQA_KV_HEADS, tgk_ref, _swap_axial).T.astype(BF16)
    gv_ref[0] = proj(SEC_GV, SEC_GATES).astype(BF16)
    z = proj(SEC_GATES, w_ref.shape[0])
    gates_ref[0] = (1.0 / (1.0 + jnp.exp(-z))).astype(BF16)


def _mix_in(xT, g, w_inT, tabs, *, ts_pref=512):
    B, D, S = xT.shape
    d_in = w_inT.shape[0]
    ts = _pick_tile(S, ts_pref)
    n_gates = d_in - SEC_GATES
    kvw = GQA_KV_HEADS * GQA_HEAD_DIM

    def tr(rows):
        return pl.BlockSpec((1, rows, ts), lambda b, i: (b, 0, i))

    def nat(cols):
        return pl.BlockSpec((1, ts, cols), lambda b, i: (b, i, 0))

    tab = pl.BlockSpec((2, HEAD, ts), lambda b, i: (0, 0, i))
    sds = jax.ShapeDtypeStruct
    return pl.pallas_call(
        _mix_in_kernel,
        grid=(B, S // ts),
        in_specs=[tr(D), _resident((D, 1)), _resident((d_in, D)), tab, tab, tab, tab],
        out_specs=[tr(SEC_DK - SEC_DQ), nat(SEC_DV - SEC_DK), tr(DIFF_WIDTH),
                   tr(GQA_WIDTH), nat(kvw), tr(kvw), tr(n_gates)],
        out_shape=[sds((B, SEC_DK - SEC_DQ, S), BF16), sds((B, S, SEC_DV - SEC_DK), BF16),
                   sds((B, DIFF_WIDTH, S), BF16), sds((B, GQA_WIDTH, S), BF16),
                   sds((B, S, kvw), BF16), sds((B, kvw, S), BF16), sds((B, n_gates, S), BF16)],
        compiler_params=_params("parallel", "parallel"),
        name="mix_in",
    )(xT, g, w_inT, *tabs)


ATTN_COLS = 512
SUM_ROWS = 16
ATTN_TRIP_TILES = 8
TK_ONLINE = 512
TK_BOUNDED = 1024
EXP2_SAFE_RANGE = 64.0


def _fold_rows(x):
    slabs = [x[r:r + 8] for r in range(0, x.shape[0], 8)]
    while len(slabs) > 1:
        slabs = [a + b for a, b in zip(slabs[::2], slabs[1::2])]
    return slabs[0]


def _key_loop(k_ref, v_ref, qs_ref, stage_ref, acc_ref, v_rows, tk, bounded):
    n_tiles = k_ref.shape[1] // tk
    assert n_tiles % 2 == 0
    G = qs_ref.shape[0]
    dv = acc_ref.shape[1] - (0 if bounded else SUM_ROWS)
    ones_rows = jnp.ones((SUM_ROWS, tk), BF16)

    def key_rows(kt):
        return kt * tk if isinstance(kt, int) else pl.multiple_of(kt * tk, tk)

    def produce(kt, buf, g):
        s = _dot(k_ref[0, pl.ds(key_rows(kt), tk), :], qs_ref[g])
        if bounded:
            p = jnp.exp2(s)
            stage_ref[buf, g] = p.astype(BF16)
            return _fold_rows(p)
        stage_ref[buf, g] = s
        return jnp.max(s, axis=0, keepdims=True)

    def step(kt, buf, state, stat, first=False, prefetch=True):
        vblk = v_ref[0, :, pl.ds(key_rows(kt), tk)]
        new_state, next_stat = [], []
        for g in range(G):
            if prefetch:
                next_stat.append(produce(kt + 1, 1 - buf, g))
            lo, hi = v_rows[g]
            if bounded:
                pv = _dot(vblk[lo:hi], stage_ref[buf, g])
                acc_ref[g] = pv if first else acc_ref[g] + pv
                new_state.append(stat[g] if first else state[g] + stat[g])
                continue
            v_ext = jnp.concatenate([vblk[lo:hi], ones_rows], axis=0)
            if first:
                m_new = stat[g]
                acc_ref[g] = _dot(v_ext, jnp.exp2(stage_ref[buf, g] - m_new).astype(BF16))
            else:
                m_new = jnp.maximum(state[g], stat[g])
                alpha = jnp.exp2(state[g] - m_new)
                p = jnp.exp2(stage_ref[buf, g] - m_new).astype(BF16)
                acc_ref[g] = alpha * acc_ref[g] + _dot(v_ext, p)
            new_state.append(m_new)
        return tuple(new_state), tuple(next_stat)

    def pair(kt, state, stat, first=False, last=False):
        state, stat = step(kt, 0, state, stat, first=first)
        return step(kt + 1, 1, state, stat, prefetch=not last)

    carry = pair(0, None, tuple(produce(0, 0, g) for g in range(G)), first=True, last=n_tiles == 2)
    if n_tiles > 4:
        carry = lax.fori_loop(1, n_tiles // 2 - 1, lambda u, c: pair(2 * u, *c), carry,
                              unroll=max(1, ATTN_TRIP_TILES // (2 * G)))
    if n_tiles > 2:
        carry = pair(n_tiles - 2, *carry, last=True)
    if bounded:
        return [jnp.sum(part, axis=0, keepdims=True) for part in carry[0]]
    return [acc_ref[g, dv:dv + 1, :] for g in range(G)]


def _attn_scratch(n_groups, q_rows, dv, tk, cols, bounded):
    return [pltpu.VMEM((n_groups, q_rows, cols), BF16),
            pltpu.VMEM((2, n_groups, tk, cols), BF16 if bounded else F32),
            pltpu.VMEM((n_groups, dv + (0 if bounded else SUM_ROWS), cols), F32)]


def _diff_attn_kernel(q_ref, k_ref, v_ref, lam_ref, sub_ref, o_ref, qs_ref, stage_ref, acc_ref, *,
                      tk, lam_init, bounded):
    tq = q_ref.shape[2]
    cols = qs_ref.shape[2]
    nb = tq // cols
    q = q_ref[0]
    first_half = lax.broadcasted_iota(jnp.int32, q.shape, 0) < HEAD
    zero = jnp.zeros_like(q)
    q_maps = (jnp.where(first_half, q, zero), jnp.where(first_half, zero, q))
    for mp in range(2):
        for cblk in range(nb):
            qs_ref[mp * nb + cblk] = q_maps[mp][:, cblk * cols:(cblk + 1) * cols]

    dv = v_ref.shape[1]
    ls = _key_loop(k_ref, v_ref, qs_ref, stage_ref, acc_ref, [(0, dv)] * (2 * nb), tk, bounded)

    lam_p = lam_ref[...]
    lam = (jnp.exp(jnp.sum(lam_p[0:1] * lam_p[1:2], axis=1, keepdims=True))
           - jnp.exp(jnp.sum(lam_p[2:3] * lam_p[3:4], axis=1, keepdims=True)) + lam_init)
    for cblk in range(nb):
        d = (acc_ref[cblk, :dv, :] / ls[cblk]
             - lam * (acc_ref[nb + cblk, :dv, :] / ls[nb + cblk]))
        o_ref[0, :, cblk * cols:(cblk + 1) * cols] = (
            d * _rms_scale(d) * sub_ref[...] * (1.0 - lam_init)).astype(BF16)


def _diff_attn(dqT, dk, dvT, lam_p, sub_g, *, lam_init, bounded, tq_pref=1024):
    B, _, S = dqT.shape
    tq = _pick_tile(S, tq_pref)
    tk = _pick_tile(S, TK_BOUNDED if bounded else TK_ONLINE)
    cols = _pick_tile(tq, ATTN_COLS)
    hv = DIFF_V_DIM
    return pl.pallas_call(
        functools.partial(_diff_attn_kernel, tk=tk, lam_init=lam_init, bounded=bounded),
        grid=(B, DIFF_HEADS, S // tq),
        in_specs=[pl.BlockSpec((1, 2 * DIFF_HEAD_DIM, tq), lambda b, h, i: (b, h, i)),
                  pl.BlockSpec((1, S, 2 * DIFF_HEAD_DIM), lambda b, h, i: (b, 0, h)),
                  pl.BlockSpec((1, hv, S), lambda b, h, i: (b, h, 0)),
                  _resident(lam_p.shape), _resident(sub_g.shape)],
        out_specs=pl.BlockSpec((1, hv, tq), lambda b, h, i: (b, h, i)),
        out_shape=jax.ShapeDtypeStruct((B, DIFF_WIDTH, S), BF16),
        scratch_shapes=_attn_scratch(2 * (tq // cols), 2 * DIFF_HEAD_DIM, hv, tk, cols, bounded),
        compiler_params=_params("parallel", "parallel", "parallel"),
        name="diff_attn_bounded" if bounded else "diff_attn",
    )(dqT, dk, dvT, lam_p, sub_g)


def _gqa_attn_kernel(q_ref, k_ref, v_ref, o_ref, qs_ref, stage_ref, acc_ref, *, tk, bounded):
    d = GQA_HEAD_DIM
    tq = q_ref.shape[2]
    hp = qs_ref.shape[2] // tq
    zeros = jnp.zeros((d, hp * tq), BF16)
    v_rows = []
    for g in range(GQA_HEADS // hp):
        kv = (g * hp) // GQA_GROUP
        qh = jnp.concatenate([q_ref[0, j * d:(j + 1) * d, :] for j in range(g * hp, (g + 1) * hp)], axis=1)
        qs_ref[g] = jnp.concatenate([qh, zeros] if kv == 0 else [zeros, qh], axis=0)
        v_rows.append((kv * d, (kv + 1) * d))

    ls = _key_loop(k_ref, v_ref, qs_ref, stage_ref, acc_ref, v_rows, tk, bounded)
    for g in range(GQA_HEADS // hp):
        o = acc_ref[g, :d, :] / ls[g]
        for r in range(hp):
            j = g * hp + r
            o_ref[0, j * d:(j + 1) * d, :] = o[:, r * tq:(r + 1) * tq].astype(BF16)


def _gqa_attn(gqT, gk, gvT, *, bounded, tq_pref=512):
    B, _, S = gqT.shape
    tq = _pick_tile(S, tq_pref)
    tk = _pick_tile(S, TK_BOUNDED if bounded else TK_ONLINE)
    kvw = GQA_KV_HEADS * GQA_HEAD_DIM
    hp = max(1, ATTN_COLS // tq)
    assert GQA_GROUP % hp == 0
    return pl.pallas_call(
        functools.partial(_gqa_attn_kernel, tk=tk, bounded=bounded),
        grid=(B, S // tq),
        in_specs=[pl.BlockSpec((1, GQA_WIDTH, tq), lambda b, i: (b, 0, i)),
                  pl.BlockSpec((1, S, kvw), lambda b, i: (b, 0, 0)),
                  pl.BlockSpec((1, kvw, S), lambda b, i: (b, 0, 0))],
        out_specs=pl.BlockSpec((1, GQA_WIDTH, tq), lambda b, i: (b, 0, i)),
        out_shape=jax.ShapeDtypeStruct((B, GQA_WIDTH, S), BF16),
        scratch_shapes=_attn_scratch(GQA_HEADS // hp, kvw, GQA_HEAD_DIM, tk, hp * tq, bounded),
        compiler_params=_params("parallel", "parallel"),
        name="gqa_attn_bounded" if bounded else "gqa_attn",
    )(gqT, gk, gvT)


def _score_bound(q_gain, k_gain, q_scale):
    return 1.02 * q_scale * HEAD * jnp.max(jnp.abs(q_gain)) * jnp.max(jnp.abs(k_gain))


def _mix_out_kernel(x_ref, a_ref, b_ref, gates_ref, wa_ref, wb_ref, wo_ref, o_ref):
    D = x_ref.shape[1]
    a = _dot(wa_ref[...], a_ref[0])
    b = _dot(wb_ref[...], b_ref[0])
    merged = gates_ref[0, :D, :].astype(F32) * a + gates_ref[0, D:, :].astype(F32) * b
    o_ref[0] = x_ref[0] + _dot(wo_ref[...], merged.astype(BF16))


def _mix_out(xT, aT, bT, gatesT, waT, wbT, woT, *, ts_pref=512):
    B, D, S = xT.shape
    ts = _pick_tile(S, ts_pref)

    def tr(rows):
        return pl.BlockSpec((1, rows, ts), lambda b, i: (b, 0, i))

    return pl.pallas_call(
        _mix_out_kernel,
        grid=(B, S // ts),
        in_specs=[tr(D), tr(aT.shape[1]), tr(bT.shape[1]), tr(gatesT.shape[1]),
                  _resident(waT.shape), _resident(wbT.shape), _resident(woT.shape)],
        out_specs=tr(D),
        out_shape=jax.ShapeDtypeStruct((B, D, S), F32),
        compiler_params=_params("parallel", "parallel"),
        name="mix_out",
    )(xT, aT, bT, gatesT, waT, wbT, woT)


def _inv_freq(dim):
    return ROPE_THETA ** (-jnp.arange(0, dim, 2, dtype=F32) / dim)


def _rope_tables(S):
    t = jnp.arange(S, dtype=F32)
    row = jnp.floor_divide(jnp.arange(S), GRID_W).astype(F32)
    col = jnp.remainder(jnp.arange(S), GRID_W).astype(F32)
    ang_t = (t[:, None] * _inv_freq(DIFF_HEAD_DIM)[None, :]).T
    ang_r = (row[:, None] * _inv_freq(GQA_AXIS_DIM)[None, :]).T
    ang_c = (col[:, None] * _inv_freq(GQA_AXIS_DIM)[None, :]).T
    cos_d = jnp.concatenate([jnp.cos(ang_t)] * 2, axis=0)
    sin_d = jnp.concatenate([-jnp.sin(ang_t), jnp.sin(ang_t)], axis=0)
    cos_a = jnp.concatenate([jnp.cos(ang_r)] * 2 + [jnp.cos(ang_c)] * 2, axis=0)
    sin_a = jnp.concatenate([-jnp.sin(ang_r), jnp.sin(ang_r), -jnp.sin(ang_c), jnp.sin(ang_c)], axis=0)
    return (cos_d, sin_d), (cos_a, sin_a)


def _gain_table(gain, cos_sin, swap, scale):
    cos, sin = cos_sin
    g = gain.astype(F32)[:, None] * scale
    return jnp.stack([g * cos, swap(g) * sin], axis=0)


def _layer(xT, S, p, lam_init, ropes, first, last):
    (ffn1_norm, ffn1_w_gu, ffn1_w_down, mix_norm, w_in, diff_q_norm, diff_k_norm,
     lambda_q1, lambda_k1, lambda_q2, lambda_k2, diff_subln, gqa_q_norm, gqa_k_norm,
     w_up_diff, w_up_gqa, w_out, ffn2_norm, ffn2_w_gu, ffn2_w_down) = p
    rope_d, rope_a = ropes

    def col(v):
        return v.astype(F32)[:, None]

    def ffn_weights(w_gu, w_down):
        f = w_down.shape[0]
        return (w_gu[:, :f].T.astype(BF16), w_gu[:, f:].T.astype(BF16), w_down.T.astype(BF16))

    xT = _ffn(xT, col(ffn1_norm), *ffn_weights(ffn1_w_gu, ffn1_w_down), in_natural=first)

    diff_scale = LOG2E / math.sqrt(DIFF_HEAD_DIM)
    gqa_scale = LOG2E / math.sqrt(GQA_HEAD_DIM)
    tabs = (_gain_table(diff_q_norm, rope_d, _swap_diff, diff_scale),
            _gain_table(diff_k_norm, rope_d, _swap_diff, 1.0),
            _gain_table(gqa_q_norm, rope_a, _swap_axial, gqa_scale),
            _gain_table(gqa_k_norm, rope_a, _swap_axial, 1.0))
    dqT, dk, dvT, gqT, gk, gvT, gatesT = _mix_in(xT, col(mix_norm), w_in.T.astype(BF16), tabs)

    lam_p = jnp.stack([lambda_q1, lambda_k1, lambda_q2, lambda_k2], axis=0).astype(F32)
    aT = lax.cond(_score_bound(diff_q_norm, diff_k_norm, diff_scale) <= EXP2_SAFE_RANGE,
                  functools.partial(_diff_attn, lam_init=lam_init, bounded=True),
                  functools.partial(_diff_attn, lam_init=lam_init, bounded=False),
                  dqT, dk, dvT, lam_p, col(diff_subln))
    bT = lax.cond(_score_bound(gqa_q_norm, gqa_k_norm, gqa_scale) <= EXP2_SAFE_RANGE,
                  functools.partial(_gqa_attn, bounded=True),
                  functools.partial(_gqa_attn, bounded=False),
                  gqT, gk, gvT)
    xT = _mix_out(xT, aT, bT, gatesT, w_up_diff.T.astype(BF16), w_up_gqa.T.astype(BF16),
                  w_out.T.astype(BF16))
    return _ffn(xT, col(ffn2_norm), *ffn_weights(ffn2_w_gu, ffn2_w_down), out_natural=last)


def kernel(x_prompt, x_sample, ffn1_norm, ffn1_w_gu, ffn1_w_down, mix_norm, w_in, diff_q_norm, diff_k_norm, lambda_q1, lambda_k1, lambda_q2, lambda_k2, diff_subln, gqa_q_norm, gqa_k_norm, w_up_diff, w_up_gqa, w_out, ffn2_norm, ffn2_w_gu, ffn2_w_down):
    stacked = (ffn1_norm, ffn1_w_gu, ffn1_w_down, mix_norm, w_in, diff_q_norm, diff_k_norm,
               lambda_q1, lambda_k1, lambda_q2, lambda_k2, diff_subln, gqa_q_norm, gqa_k_norm,
               w_up_diff, w_up_gqa, w_out, ffn2_norm, ffn2_w_gu, ffn2_w_down)
    depth = ffn1_norm.shape[0]
    outs = []
    for x in (x_prompt, x_sample):
        S = x.shape[1]
        ropes = _rope_tables(S)
        y = x
        for l in range(depth):
            lam_init = 0.8 - 0.6 * math.exp(-0.3 * l)
            y = _layer(y, S, tuple(w[l] for w in stacked), lam_init, ropes,
                       first=(l == 0), last=(l == depth - 1))
        outs.append(y)
    return tuple(outs)
```

```python
import functools
import math
from typing import NamedTuple

import jax
import jax.numpy as jnp
from jax import lax
from jax.experimental import pallas as pl
from jax.experimental.pallas import tpu as pltpu

GRID_W = 64
ROPE_THETA = 10000.0
EPS = 1e-6
DIFF_HEADS = 4
DIFF_HEAD_DIM = 64
DIFF_V_DIM = 2 * DIFF_HEAD_DIM
DIFF_WIDTH = DIFF_HEADS * DIFF_V_DIM
GQA_HEADS = 8
GQA_KV_HEADS = 2
GQA_GROUP = GQA_HEADS // GQA_KV_HEADS
GQA_HEAD_DIM = 64
GQA_AXIS_DIM = GQA_HEAD_DIM // 2
GQA_WIDTH = GQA_HEADS * GQA_HEAD_DIM
HEAD = 64
LOG2E = math.log2(math.e)

SEC_DQ = 0
SEC_DK = SEC_DQ + DIFF_HEADS * 2 * DIFF_HEAD_DIM
SEC_DV = SEC_DK + DIFF_HEADS * 2 * DIFF_HEAD_DIM
SEC_GQ = SEC_DV + DIFF_WIDTH
SEC_GK = SEC_GQ + GQA_WIDTH
SEC_GV = SEC_GK + GQA_KV_HEADS * GQA_HEAD_DIM
SEC_GATES = SEC_GV + GQA_KV_HEADS * GQA_HEAD_DIM

VMEM_LIMIT_BYTES = 56 * 1024 * 1024

BF16 = jnp.bfloat16
F32 = jnp.float32


def _pick_tile(n, pref):
    t = min(n, pref)
    while n % t:
        t //= 2
    return t


def _params(*sem):
    return pltpu.CompilerParams(dimension_semantics=sem, vmem_limit_bytes=VMEM_LIMIT_BYTES)


def _resident(shape):
    nd = len(shape)
    return pl.BlockSpec(shape, lambda *_: (0,) * nd, pipeline_mode=pl.Buffered(1))


def _dot(a, b):
    return jnp.dot(a, b, preferred_element_type=F32)


def _rms_scale(x):
    return lax.rsqrt(jnp.mean(x * x, axis=0, keepdims=True) + EPS)


def _ffn_kernel(x_ref, g_ref, wg_ref, wu_ref, wd_ref, o_ref, *, in_natural, out_natural):
    x = x_ref[0]
    if in_natural:
        x = x.T
    h = (x * _rms_scale(x) * g_ref[...]).astype(BF16)
    gate = _dot(wg_ref[...], h)
    up = _dot(wu_ref[...], h)
    act = (gate * (1.0 / (1.0 + jnp.exp(-gate))) * up).astype(BF16)
    y = x + 0.5 * _dot(wd_ref[...], act)
    o_ref[0] = y.T if out_natural else y


def _ffn(x, g, wgT, wuT, wdT, *, in_natural=False, out_natural=False, ts_pref=512):
    if in_natural:
        B, S, D = x.shape
    else:
        B, D, S = x.shape
    F = wgT.shape[0]
    ts = _pick_tile(S, ts_pref)
    nat = pl.BlockSpec((1, ts, D), lambda b, i: (b, i, 0))
    tr = pl.BlockSpec((1, D, ts), lambda b, i: (b, 0, i))
    out_shape = (B, S, D) if out_natural else (B, D, S)
    return pl.pallas_call(
        functools.partial(_ffn_kernel, in_natural=in_natural, out_natural=out_natural),
        grid=(B, S // ts),
        in_specs=[nat if in_natural else tr, _resident((D, 1)),
                  _resident((F, D)), _resident((F, D)), _resident((D, F))],
        out_specs=nat if out_natural else tr,
        out_shape=jax.ShapeDtypeStruct(out_shape, F32),
        compiler_params=_params("parallel", "parallel"),
        name="ffn",
    )(x, g, wgT, wuT, wdT)


def _swap_diff(x):
    return jnp.concatenate([x[HEAD // 2:], x[:HEAD // 2]], axis=0)


def _swap_axial(x):
    q = HEAD // 4
    return jnp.concatenate([x[q:2 * q], x[:q], x[3 * q:], x[2 * q:3 * q]], axis=0)


def _norm_rope(u, n_groups, tab_ref, swap):
    ta = tab_ref[0]
    tb = tab_ref[1]
    outs = []
    for j in range(n_groups):
        x = u[j * HEAD:(j + 1) * HEAD]
        outs.append(_rms_scale(x) * (x * ta + swap(x) * tb))
    return jnp.concatenate(outs, axis=0)


def _mix_in_kernel(x_ref, g_ref, w_ref, tdq_ref, tdk_ref, tgq_ref, tgk_ref,
                   dq_ref, dk_ref, dv_ref, gq_ref, gk_ref, gv_ref, gates_ref):
    x = x_ref[0]
    h = (x * _rms_scale(x) * g_ref[...]).astype(BF16)

    def proj(lo, hi):
        return _dot(w_ref[lo:hi, :], h)

    n_dqk = DIFF_HEADS * 2
    dq_ref[0] = _norm_rope(proj(SEC_DQ, SEC_DK), n_dqk, tdq_ref, _swap_diff).astype(BF16)
    dk_ref[0] = _norm_rope(proj(SEC_DK, SEC_DV), n_dqk, tdk_ref, _swap_diff).T.astype(BF16)
    dv_ref[0] = proj(SEC_DV, SEC_GQ).astype(BF16)
    gq_ref[0] = _norm_rope(proj(SEC_GQ, SEC_GK), GQA_HEADS, tgq_ref, _swap_axial).astype(BF16)
    gk_ref[0] = _norm_rope(proj(SEC_GK, SEC_GV), GQA_KV_HEADS, tgk_ref, _swap_axial).T.astype(BF16)
    gv_ref[0] = proj(SEC_GV, SEC_GATES).astype(BF16)
    z = proj(SEC_GATES, w_ref.shape[0])
    gates_ref[0] = (1.0 / (1.0 + jnp.exp(-z))).astype(BF16)


def _mix_in(xT, g, w_inT, tabs, *, ts_pref=256):
    B, D, S = xT.shape
    d_in = w_inT.shape[0]
    ts = _pick_tile(S, ts_pref)
    n_gates = d_in - SEC_GATES
    kvw = GQA_KV_HEADS * GQA_HEAD_DIM

    def tr(rows):
        return pl.BlockSpec((1, rows, ts), lambda b, i: (b, 0, i))

    def nat(cols):
        return pl.BlockSpec((1, ts, cols), lambda b, i: (b, i, 0))

    tab = pl.BlockSpec((2, HEAD, ts), lambda b, i: (0, 0, i))
    sds = jax.ShapeDtypeStruct
    return pl.pallas_call(
        _mix_in_kernel,
        grid=(B, S // ts),
        in_specs=[tr(D), _resident((D, 1)), _resident((d_in, D)), tab, tab, tab, tab],
        out_specs=[tr(SEC_DK - SEC_DQ), nat(SEC_DV - SEC_DK), tr(DIFF_WIDTH),
                   tr(GQA_WIDTH), nat(kvw), tr(kvw), tr(n_gates)],
        out_shape=[sds((B, SEC_DK - SEC_DQ, S), BF16), sds((B, S, SEC_DV - SEC_DK), BF16),
                   sds((B, DIFF_WIDTH, S), BF16), sds((B, GQA_WIDTH, S), BF16),
                   sds((B, S, kvw), BF16), sds((B, kvw, S), BF16), sds((B, n_gates, S), BF16)],
        compiler_params=_params("parallel", "parallel"),
        name="mix_in",
    )(xT, g, w_inT, *tabs)


SUM_ROWS = 16
EXP2_SAFE_RANGE = 64.0


class _AttnTiles(NamedTuple):
    cols: int
    tk: int
    tq_diff: int
    tq_gqa: int
    trip_units: int


ONLINE_TILES = _AttnTiles(cols=512, tk=512, tq_diff=1024, tq_gqa=512, trip_units=16)
BOUNDED_TILES = _AttnTiles(cols=1024, tk=1024, tq_diff=2048, tq_gqa=512, trip_units=4)


def _fold_rows(x):
    slabs = [x[r:r + 8] for r in range(0, x.shape[0], 8)]
    while len(slabs) > 1:
        slabs = [a + b for a, b in zip(slabs[::2], slabs[1::2])]
    return slabs[0]


def _key_loop(k_ref, v_ref, qs_ref, stage_ref, acc_ref, v_rows, tk, bounded):
    n_tiles = k_ref.shape[1] // tk
    assert n_tiles % 2 == 0
    G = qs_ref.shape[0]
    trip_units = (BOUNDED_TILES if bounded else ONLINE_TILES).trip_units
    dv = acc_ref.shape[1] - (0 if bounded else SUM_ROWS)
    ones_rows = jnp.ones((SUM_ROWS, tk), BF16)

    def key_rows(kt):
        return kt * tk if isinstance(kt, int) else pl.multiple_of(kt * tk, tk)

    def produce(kt, buf, g):
        s = _dot(k_ref[0, pl.ds(key_rows(kt), tk), :], qs_ref[g])
        if bounded:
            p = jnp.exp2(s)
            stage_ref[buf, g] = p.astype(BF16)
            return _fold_rows(p)
        stage_ref[buf, g] = s
        return jnp.max(s, axis=0, keepdims=True)

    def step(kt, buf, state, stat, first=False, prefetch=True):
        vblk = v_ref[0, :, pl.ds(key_rows(kt), tk)]
        new_state, next_stat = [], []
        for g in range(G):
            if prefetch:
                next_stat.append(produce(kt + 1, 1 - buf, g))
            lo, hi = v_rows[g]
            if bounded:
                pv = _dot(vblk[lo:hi], stage_ref[buf, g])
                acc_ref[g] = pv if first else acc_ref[g] + pv
                new_state.append(stat[g] if first else state[g] + stat[g])
                continue
            v_ext = jnp.concatenate([vblk[lo:hi], ones_rows], axis=0)
            if first:
                m_new = stat[g]
                acc_ref[g] = _dot(v_ext, jnp.exp2(stage_ref[buf, g] - m_new).astype(BF16))
            else:
                m_new = jnp.maximum(state[g], stat[g])
                alpha = jnp.exp2(state[g] - m_new)
                p = jnp.exp2(stage_ref[buf, g] - m_new).astype(BF16)
                acc_ref[g] = alpha * acc_ref[g] + _dot(v_ext, p)
            new_state.append(m_new)
        return tuple(new_state), tuple(next_stat)

    def pair(kt, state, stat, first=False, last=False):
        state, stat = step(kt, 0, state, stat, first=first)
        return step(kt + 1, 1, state, stat, prefetch=not last)

    carry = pair(0, None, tuple(produce(0, 0, g) for g in range(G)), first=True, last=n_tiles == 2)
    if n_tiles > 4:
        carry = lax.fori_loop(1, n_tiles // 2 - 1, lambda u, c: pair(2 * u, *c), carry,
                              unroll=max(1, trip_units // (2 * G)))
    if n_tiles > 2:
        carry = pair(n_tiles - 2, *carry, last=True)
    if bounded:
        return [jnp.sum(part, axis=0, keepdims=True) for part in carry[0]]
    return [acc_ref[g, dv:dv + 1, :] for g in range(G)]


def _attn_scratch(n_groups, q_rows, dv, tk, cols, bounded):
    return [pltpu.VMEM((n_groups, q_rows, cols), BF16),
            pltpu.VMEM((2, n_groups, tk, cols), BF16 if bounded else F32),
            pltpu.VMEM((n_groups, dv + (0 if bounded else SUM_ROWS), cols), F32)]


def _diff_attn_kernel(q_ref, k_ref, v_ref, lam_ref, sub_ref, o_ref, qs_ref, stage_ref, acc_ref, *,
                      tk, lam_init, bounded):
    tq = q_ref.shape[2]
    cols = qs_ref.shape[2]
    nb = tq // cols
    q = q_ref[0]
    first_half = lax.broadcasted_iota(jnp.int32, q.shape, 0) < HEAD
    zero = jnp.zeros_like(q)
    q_maps = (jnp.where(first_half, q, zero), jnp.where(first_half, zero, q))
    for mp in range(2):
        for cblk in range(nb):
            qs_ref[mp * nb + cblk] = q_maps[mp][:, cblk * cols:(cblk + 1) * cols]

    dv = v_ref.shape[1]
    ls = _key_loop(k_ref, v_ref, qs_ref, stage_ref, acc_ref, [(0, dv)] * (2 * nb), tk, bounded)

    lam_p = lam_ref[...]
    lam = (jnp.exp(jnp.sum(lam_p[0:1] * lam_p[1:2], axis=1, keepdims=True))
           - jnp.exp(jnp.sum(lam_p[2:3] * lam_p[3:4], axis=1, keepdims=True)) + lam_init)
    for cblk in range(nb):
        d = (acc_ref[cblk, :dv, :] / ls[cblk]
             - lam * (acc_ref[nb + cblk, :dv, :] / ls[nb + cblk]))
        o_ref[0, :, cblk * cols:(cblk + 1) * cols] = (
            d * _rms_scale(d) * sub_ref[...] * (1.0 - lam_init)).astype(BF16)


def _diff_attn(dqT, dk, dvT, lam_p, sub_g, *, lam_init, bounded):
    B, _, S = dqT.shape
    tiles = BOUNDED_TILES if bounded else ONLINE_TILES
    tq = _pick_tile(S, tiles.tq_diff)
    tk = _pick_tile(S, tiles.tk)
    cols = _pick_tile(tq, tiles.cols)
    hv = DIFF_V_DIM
    return pl.pallas_call(
        functools.partial(_diff_attn_kernel, tk=tk, lam_init=lam_init, bounded=bounded),
        grid=(B, DIFF_HEADS, S // tq),
        in_specs=[pl.BlockSpec((1, 2 * DIFF_HEAD_DIM, tq), lambda b, h, i: (b, h, i)),
                  pl.BlockSpec((1, S, 2 * DIFF_HEAD_DIM), lambda b, h, i: (b, 0, h)),
                  pl.BlockSpec((1, hv, S), lambda b, h, i: (b, h, 0)),
                  _resident(lam_p.shape), _resident(sub_g.shape)],
        out_specs=pl.BlockSpec((1, hv, tq), lambda b, h, i: (b, h, i)),
        out_shape=jax.ShapeDtypeStruct((B, DIFF_WIDTH, S), BF16),
        scratch_shapes=_attn_scratch(2 * (tq // cols), 2 * DIFF_HEAD_DIM, hv, tk, cols, bounded),
        compiler_params=_params("parallel", "parallel", "parallel"),
        name="diff_attn_bounded" if bounded else "diff_attn",
    )(dqT, dk, dvT, lam_p, sub_g)


def _gqa_attn_kernel(q_ref, k_ref, v_ref, o_ref, qs_ref, stage_ref, acc_ref, *, tk, bounded):
    d = GQA_HEAD_DIM
    tq = q_ref.shape[2]
    hp = qs_ref.shape[2] // tq
    zeros = jnp.zeros((d, hp * tq), BF16)
    v_rows = []
    for g in range(GQA_HEADS // hp):
        kv = (g * hp) // GQA_GROUP
        qh = jnp.concatenate([q_ref[0, j * d:(j + 1) * d, :] for j in range(g * hp, (g + 1) * hp)], axis=1)
        qs_ref[g] = jnp.concatenate([qh, zeros] if kv == 0 else [zeros, qh], axis=0)
        v_rows.append((kv * d, (kv + 1) * d))

    ls = _key_loop(k_ref, v_ref, qs_ref, stage_ref, acc_ref, v_rows, tk, bounded)
    for g in range(GQA_HEADS // hp):
        o = acc_ref[g, :d, :] / ls[g]
        for r in range(hp):
            j = g * hp + r
            o_ref[0, j * d:(j + 1) * d, :] = o[:, r * tq:(r + 1) * tq].astype(BF16)


def _gqa_attn(gqT, gk, gvT, *, bounded):
    B, _, S = gqT.shape
    tiles = BOUNDED_TILES if bounded else ONLINE_TILES
    tq = _pick_tile(S, tiles.tq_gqa)
    tk = _pick_tile(S, tiles.tk)
    kvw = GQA_KV_HEADS * GQA_HEAD_DIM
    hp = max(1, tiles.cols // tq)
    assert GQA_GROUP % hp == 0
    return pl.pallas_call(
        functools.partial(_gqa_attn_kernel, tk=tk, bounded=bounded),
        grid=(B, S // tq),
        in_specs=[pl.BlockSpec((1, GQA_WIDTH, tq), lambda b, i: (b, 0, i)),
                  pl.BlockSpec((1, S, kvw), lambda b, i: (b, 0, 0)),
                  pl.BlockSpec((1, kvw, S), lambda b, i: (b, 0, 0))],
        out_specs=pl.BlockSpec((1, GQA_WIDTH, tq), lambda b, i: (b, 0, i)),
        out_shape=jax.ShapeDtypeStruct((B, GQA_WIDTH, S), BF16),
        scratch_shapes=_attn_scratch(GQA_HEADS // hp, kvw, GQA_HEAD_DIM, tk, hp * tq, bounded),
        compiler_params=_params("parallel", "parallel"),
        name="gqa_attn_bounded" if bounded else "gqa_attn",
    )(gqT, gk, gvT)


def _score_bound(q_gain, k_gain, q_scale):
    return 1.02 * q_scale * HEAD * jnp.max(jnp.abs(q_gain)) * jnp.max(jnp.abs(k_gain))


def _mix_out_kernel(x_ref, a_ref, b_ref, gates_ref, wa_ref, wb_ref, wo_ref, o_ref):
    D = x_ref.shape[1]
    a = _dot(wa_ref[...], a_ref[0])
    b = _dot(wb_ref[...], b_ref[0])
    merged = gates_ref[0, :D, :].astype(F32) * a + gates_ref[0, D:, :].astype(F32) * b
    o_ref[0] = x_ref[0] + _dot(wo_ref[...], merged.astype(BF16))


def _mix_out(xT, aT, bT, gatesT, waT, wbT, woT, *, ts_pref=1024):
    B, D, S = xT.shape
    ts = _pick_tile(S, ts_pref)

    def tr(rows):
        return pl.BlockSpec((1, rows, ts), lambda b, i: (b, 0, i))

    return pl.pallas_call(
        _mix_out_kernel,
        grid=(B, S // ts),
        in_specs=[tr(D), tr(aT.shape[1]), tr(bT.shape[1]), tr(gatesT.shape[1]),
                  _resident(waT.shape), _resident(wbT.shape), _resident(woT.shape)],
        out_specs=tr(D),
        out_shape=jax.ShapeDtypeStruct((B, D, S), F32),
        compiler_params=_params("parallel", "parallel"),
        name="mix_out",
    )(xT, aT, bT, gatesT, waT, wbT, woT)


def _inv_freq(dim):
    return ROPE_THETA ** (-jnp.arange(0, dim, 2, dtype=F32) / dim)


def _rope_tables(S):
    t = jnp.arange(S, dtype=F32)
    row = jnp.floor_divide(jnp.arange(S), GRID_W).astype(F32)
    col = jnp.remainder(jnp.arange(S), GRID_W).astype(F32)
    ang_t = (t[:, None] * _inv_freq(DIFF_HEAD_DIM)[None, :]).T
    ang_r = (row[:, None] * _inv_freq(GQA_AXIS_DIM)[None, :]).T
    ang_c = (col[:, None] * _inv_freq(GQA_AXIS_DIM)[None, :]).T
    cos_d = jnp.concatenate([jnp.cos(ang_t)] * 2, axis=0)
    sin_d = jnp.concatenate([-jnp.sin(ang_t), jnp.sin(ang_t)], axis=0)
    cos_a = jnp.concatenate([jnp.cos(ang_r)] * 2 + [jnp.cos(ang_c)] * 2, axis=0)
    sin_a = jnp.concatenate([-jnp.sin(ang_r), jnp.sin(ang_r), -jnp.sin(ang_c), jnp.sin(ang_c)], axis=0)
    return (cos_d, sin_d), (cos_a, sin_a)


def _gain_table(gain, cos_sin, swap, scale):
    cos, sin = cos_sin
    g = gain.astype(F32)[:, None] * scale
    return jnp.stack([g * cos, swap(g) * sin], axis=0)


def _layer(xT, S, p, lam_init, ropes, first, last):
    (ffn1_norm, ffn1_w_gu, ffn1_w_down, mix_norm, w_in, diff_q_norm, diff_k_norm,
     lambda_q1, lambda_k1, lambda_q2, lambda_k2, diff_subln, gqa_q_norm, gqa_k_norm,
     w_up_diff, w_up_gqa, w_out, ffn2_norm, ffn2_w_gu, ffn2_w_down) = p
    rope_d, rope_a = ropes

    def col(v):
        return v.astype(F32)[:, None]

    def ffn_weights(w_gu, w_down):
        f = w_down.shape[0]
        return (w_gu[:, :f].T.astype(BF16), w_gu[:, f:].T.astype(BF16), w_down.T.astype(BF16))

    xT = _ffn(xT, col(ffn1_norm), *ffn_weights(ffn1_w_gu, ffn1_w_down), in_natural=first)

    diff_scale = LOG2E / math.sqrt(DIFF_HEAD_DIM)
    gqa_scale = LOG2E / math.sqrt(GQA_HEAD_DIM)
    tabs = (_gain_table(diff_q_norm, rope_d, _swap_diff, diff_scale),
            _gain_table(diff_k_norm, rope_d, _swap_diff, 1.0),
            _gain_table(gqa_q_norm, rope_a, _swap_axial, gqa_scale),
            _gain_table(gqa_k_norm, rope_a, _swap_axial, 1.0))
    dqT, dk, dvT, gqT, gk, gvT, gatesT = _mix_in(xT, col(mix_norm), w_in.T.astype(BF16), tabs)

    lam_p = jnp.stack([lambda_q1, lambda_k1, lambda_q2, lambda_k2], axis=0).astype(F32)
    aT = lax.cond(_score_bound(diff_q_norm, diff_k_norm, diff_scale) <= EXP2_SAFE_RANGE,
                  functools.partial(_diff_attn, lam_init=lam_init, bounded=True),
                  functools.partial(_diff_attn, lam_init=lam_init, bounded=False),
                  dqT, dk, dvT, lam_p, col(diff_subln))
    bT = lax.cond(_score_bound(gqa_q_norm, gqa_k_norm, gqa_scale) <= EXP2_SAFE_RANGE,
                  functools.partial(_gqa_attn, bounded=True),
                  functools.partial(_gqa_attn, bounded=False),
                  gqT, gk, gvT)
    xT = _mix_out(xT, aT, bT, gatesT, w_up_diff.T.astype(BF16), w_up_gqa.T.astype(BF16),
                  w_out.T.astype(BF16))
    return _ffn(xT, col(ffn2_norm), *ffn_weights(ffn2_w_gu, ffn2_w_down), out_natural=last)


def kernel(x_prompt, x_sample, ffn1_norm, ffn1_w_gu, ffn1_w_down, mix_norm, w_in, diff_q_norm, diff_k_norm, lambda_q1, lambda_k1, lambda_q2, lambda_k2, diff_subln, gqa_q_norm, gqa_k_norm, w_up_diff, w_up_gqa, w_out, ffn2_norm, ffn2_w_gu, ffn2_w_down):
    stacked = (ffn1_norm, ffn1_w_gu, ffn1_w_down, mix_norm, w_in, diff_q_norm, diff_k_norm,
               lambda_q1, lambda_k1, lambda_q2, lambda_k2, diff_subln, gqa_q_norm, gqa_k_norm,
               w_up_diff, w_up_gqa, w_out, ffn2_norm, ffn2_w_gu, ffn2_w_down)
    depth = ffn1_norm.shape[0]
    outs = []
    for x in (x_prompt, x_sample):
        S = x.shape[1]
        ropes = _rope_tables(S)
        y = x
        for l in range(depth):
            lam_init = 0.8 - 0.6 * math.exp(-0.3 * l)
            y = _layer(y, S, tuple(w[l] for w in stacked), lam_init, ropes,
                       first=(l == 0), last=(l == depth - 1))
        outs.append(y)
    return tuple(outs)
```

```python
import functools
import math
from typing import NamedTuple

import jax
import jax.numpy as jnp
from jax import lax
from jax.experimental import pallas as pl
from jax.experimental.pallas import tpu as pltpu

GRID_W = 64
ROPE_THETA = 10000.0
EPS = 1e-6
DIFF_HEADS = 4
DIFF_HEAD_DIM = 64
DIFF_V_DIM = 2 * DIFF_HEAD_DIM
DIFF_WIDTH = DIFF_HEADS * DIFF_V_DIM
GQA_HEADS = 8
GQA_KV_HEADS = 2
GQA_GROUP = GQA_HEADS // GQA_KV_HEADS
GQA_HEAD_DIM = 64
GQA_AXIS_DIM = GQA_HEAD_DIM // 2
GQA_WIDTH = GQA_HEADS * GQA_HEAD_DIM
HEAD = 64
LOG2E = math.log2(math.e)

SEC_DQ = 0
SEC_DK = SEC_DQ + DIFF_HEADS * 2 * DIFF_HEAD_DIM
SEC_DV = SEC_DK + DIFF_HEADS * 2 * DIFF_HEAD_DIM
SEC_GQ = SEC_DV + DIFF_WIDTH
SEC_GK = SEC_GQ + GQA_WIDTH
SEC_GV = SEC_GK + GQA_KV_HEADS * GQA_HEAD_DIM
SEC_GATES = SEC_GV + GQA_KV_HEADS * GQA_HEAD_DIM

VMEM_LIMIT_BYTES = 56 * 1024 * 1024

BF16 = jnp.bfloat16
F32 = jnp.float32


def _pick_tile(n, pref):
    t = min(n, pref)
    while n % t:
        t //= 2
    return t


def _params(*sem):
    return pltpu.CompilerParams(dimension_semantics=sem, vmem_limit_bytes=VMEM_LIMIT_BYTES)


def _resident(shape):
    nd = len(shape)
    return pl.BlockSpec(shape, lambda *_: (0,) * nd, pipeline_mode=pl.Buffered(1))


def _dot(a, b):
    return jnp.dot(a, b, preferred_element_type=F32)


def _rms_scale(x):
    return lax.rsqrt(jnp.mean(x * x, axis=0, keepdims=True) + EPS)


def _ffn_kernel(x_ref, g_ref, wg_ref, wu_ref, wd_ref, o_ref, *, in_natural, out_natural):
    x = x_ref[0]
    if in_natural:
        x = x.T
    h = (x * _rms_scale(x) * g_ref[...]).astype(BF16)
    gate = _dot(wg_ref[...], h)
    up = _dot(wu_ref[...], h)
    act = (gate * (1.0 / (1.0 + jnp.exp(-gate))) * up).astype(BF16)
    y = x + 0.5 * _dot(wd_ref[...], act)
    o_ref[0] = y.T if out_natural else y


def _ffn(x, g, wgT, wuT, wdT, *, in_natural=False, out_natural=False, ts_pref=512):
    if in_natural:
        B, S, D = x.shape
    else:
        B, D, S = x.shape
    F = wgT.shape[0]
    ts = _pick_tile(S, ts_pref)
    nat = pl.BlockSpec((1, ts, D), lambda b, i: (b, i, 0))
    tr = pl.BlockSpec((1, D, ts), lambda b, i: (b, 0, i))
    out_shape = (B, S, D) if out_natural else (B, D, S)
    return pl.pallas_call(
        functools.partial(_ffn_kernel, in_natural=in_natural, out_natural=out_natural),
        grid=(B, S // ts),
        in_specs=[nat if in_natural else tr, _resident((D, 1)),
                  _resident((F, D)), _resident((F, D)), _resident((D, F))],
        out_specs=nat if out_natural else tr,
        out_shape=jax.ShapeDtypeStruct(out_shape, F32),
        compiler_params=_params("parallel", "parallel"),
        name="ffn",
    )(x, g, wgT, wuT, wdT)


def _swap_diff(x):
    return jnp.concatenate([x[HEAD // 2:], x[:HEAD // 2]], axis=0)


def _swap_axial(x):
    q = HEAD // 4
    return jnp.concatenate([x[q:2 * q], x[:q], x[3 * q:], x[2 * q:3 * q]], axis=0)


def _norm_rope(u, n_groups, tab_ref, swap):
    ta = tab_ref[0]
    tb = tab_ref[1]
    outs = []
    for j in range(n_groups):
        x = u[j * HEAD:(j + 1) * HEAD]
        outs.append(_rms_scale(x) * (x * ta + swap(x) * tb))
    return jnp.concatenate(outs, axis=0)


def _mix_in_kernel(x_ref, g_ref, w_ref, tdq_ref, tdk_ref, tgq_ref, tgk_ref,
                   dq_ref, dk_ref, dv_ref, gq_ref, gk_ref, gv_ref, gates_ref):
    x = x_ref[0]
    h = (x * _rms_scale(x) * g_ref[...]).astype(BF16)

    def proj(lo, hi):
        return _dot(w_ref[lo:hi, :], h)

    n_dqk = DIFF_HEADS * 2
    dq_ref[0] = _norm_rope(proj(SEC_DQ, SEC_DK), n_dqk, tdq_ref, _swap_diff).astype(BF16)
    dk_ref[0] = _norm_rope(proj(SEC_DK, SEC_DV), n_dqk, tdk_ref, _swap_diff).T.astype(BF16)
    dv_ref[0] = proj(SEC_DV, SEC_GQ).astype(BF16)
    gq_ref[0] = _norm_rope(proj(SEC_GQ, SEC_GK), GQA_HEADS, tgq_ref, _swap_axial).astype(BF16)
    gk_ref[0] = _norm_rope(proj(SEC_GK, SEC_GV), GQA_KV_HEADS, tgk_ref, _swap_axial).T.astype(BF16)
    gv_ref[0] = proj(SEC_GV, SEC_GATES).astype(BF16)
    z = proj(SEC_GATES, w_ref.shape[0])
    gates_ref[0] = (1.0 / (1.0 + jnp.exp(-z))).astype(BF16)


def _mix_in(xT, g, w_inT, tabs, *, ts_pref=256):
    B, D, S = xT.shape
    d_in = w_inT.shape[0]
    ts = _pick_tile(S, ts_pref)
    n_gates = d_in - SEC_GATES
    kvw = GQA_KV_HEADS * GQA_HEAD_DIM

    def tr(rows):
        return pl.BlockSpec((1, rows, ts), lambda b, i: (b, 0, i))

    def nat(cols):
        return pl.BlockSpec((1, ts, cols), lambda b, i: (b, i, 0))

    tab = pl.BlockSpec((2, HEAD, ts), lambda b, i: (0, 0, i))
    sds = jax.ShapeDtypeStruct
    return pl.pallas_call(
        _mix_in_kernel,
        grid=(B, S // ts),
        in_specs=[tr(D), _resident((D, 1)), _resident((d_in, D)), tab, tab, tab, tab],
        out_specs=[tr(SEC_DK - SEC_DQ), nat(SEC_DV - SEC_DK), tr(DIFF_WIDTH),
                   tr(GQA_WIDTH), nat(kvw), tr(kvw), tr(n_gates)],
        out_shape=[sds((B, SEC_DK - SEC_DQ, S), BF16), sds((B, S, SEC_DV - SEC_DK), BF16),
                   sds((B, DIFF_WIDTH, S), BF16), sds((B, GQA_WIDTH, S), BF16),
                   sds((B, S, kvw), BF16), sds((B, kvw, S), BF16), sds((B, n_gates, S), BF16)],
        compiler_params=_params("parallel", "parallel"),
        name="mix_in",
    )(xT, g, w_inT, *tabs)


SUM_ROWS = 16
EXP2_SAFE_RANGE = 64.0


class _AttnTiles(NamedTuple):
    cols: int
    tk: int
    tq_diff: int
    tq_gqa: int
    trip_units: int


ONLINE_TILES = _AttnTiles(cols=512, tk=512, tq_diff=1024, tq_gqa=512, trip_units=16)
BOUNDED_TILES = _AttnTiles(cols=2048, tk=1024, tq_diff=2048, tq_gqa=512, trip_units=4)


def _fold_rows(x):
    slabs = [x[r:r + 8] for r in range(0, x.shape[0], 8)]
    while len(slabs) > 1:
        slabs = [a + b for a, b in zip(slabs[::2], slabs[1::2])]
    return slabs[0]


def _key_loop(k_ref, v_ref, qs_ref, stage_ref, acc_ref, v_rows, tk, bounded):
    n_tiles = k_ref.shape[1] // tk
    assert n_tiles % 2 == 0
    G = qs_ref.shape[0]
    trip_units = (BOUNDED_TILES if bounded else ONLINE_TILES).trip_units
    dv = acc_ref.shape[1] - (0 if bounded else SUM_ROWS)
    ones_rows = jnp.ones((SUM_ROWS, tk), BF16)

    def key_rows(kt):
        return kt * tk if isinstance(kt, int) else pl.multiple_of(kt * tk, tk)

    def produce(kt, buf, g):
        s = _dot(k_ref[0, pl.ds(key_rows(kt), tk), :], qs_ref[g])
        if bounded:
            p = jnp.exp2(s)
            stage_ref[buf, g] = p.astype(BF16)
            return _fold_rows(p)
        stage_ref[buf, g] = s
        return jnp.max(s, axis=0, keepdims=True)

    def step(kt, buf, state, stat, first=False, prefetch=True):
        vblk = v_ref[0, :, pl.ds(key_rows(kt), tk)]
        new_state, next_stat = [], []
        for g in range(G):
            if prefetch:
                next_stat.append(produce(kt + 1, 1 - buf, g))
            lo, hi = v_rows[g]
            if bounded:
                pv = _dot(vblk[lo:hi], stage_ref[buf, g])
                acc_ref[g] = pv if first else acc_ref[g] + pv
                new_state.append(stat[g] if first else state[g] + stat[g])
                continue
            v_ext = jnp.concatenate([vblk[lo:hi], ones_rows], axis=0)
            if first:
                m_new = stat[g]
                acc_ref[g] = _dot(v_ext, jnp.exp2(stage_ref[buf, g] - m_new).astype(BF16))
            else:
                m_new = jnp.maximum(state[g], stat[g])
                alpha = jnp.exp2(state[g] - m_new)
                p = jnp.exp2(stage_ref[buf, g] - m_new).astype(BF16)
                acc_ref[g] = alpha * acc_ref[g] + _dot(v_ext, p)
            new_state.append(m_new)
        return tuple(new_state), tuple(next_stat)

    def pair(kt, state, stat, first=False, last=False):
        state, stat = step(kt, 0, state, stat, first=first)
        return step(kt + 1, 1, state, stat, prefetch=not last)

    carry = pair(0, None, tuple(produce(0, 0, g) for g in range(G)), first=True, last=n_tiles == 2)
    if n_tiles > 4:
        carry = lax.fori_loop(1, n_tiles // 2 - 1, lambda u, c: pair(2 * u, *c), carry,
                              unroll=max(1, trip_units // (2 * G)))
    if n_tiles > 2:
        carry = pair(n_tiles - 2, *carry, last=True)
    if bounded:
        return [jnp.sum(part, axis=0, keepdims=True) for part in carry[0]]
    return [acc_ref[g, dv:dv + 1, :] for g in range(G)]


def _attn_scratch(n_groups, q_rows, dv, tk, cols, bounded):
    return [pltpu.VMEM((n_groups, q_rows, cols), BF16),
            pltpu.VMEM((2, n_groups, tk, cols), BF16 if bounded else F32),
            pltpu.VMEM((n_groups, dv + (0 if bounded else SUM_ROWS), cols), F32)]


def _diff_attn_kernel(q_ref, k_ref, v_ref, lam_ref, sub_ref, o_ref, qs_ref, stage_ref, acc_ref, *,
                      tk, lam_init, bounded):
    tq = q_ref.shape[2]
    cols = qs_ref.shape[2]
    nb = tq // cols
    q = q_ref[0]
    first_half = lax.broadcasted_iota(jnp.int32, q.shape, 0) < HEAD
    zero = jnp.zeros_like(q)
    q_maps = (jnp.where(first_half, q, zero), jnp.where(first_half, zero, q))
    for mp in range(2):
        for cblk in range(nb):
            qs_ref[mp * nb + cblk] = q_maps[mp][:, cblk * cols:(cblk + 1) * cols]

    dv = v_ref.shape[1]
    ls = _key_loop(k_ref, v_ref, qs_ref, stage_ref, acc_ref, [(0, dv)] * (2 * nb), tk, bounded)

    lam_p = lam_ref[...]
    lam = (jnp.exp(jnp.sum(lam_p[0:1] * lam_p[1:2], axis=1, keepdims=True))
           - jnp.exp(jnp.sum(lam_p[2:3] * lam_p[3:4], axis=1, keepdims=True)) + lam_init)
    for cblk in range(nb):
        d = (acc_ref[cblk, :dv, :] / ls[cblk]
             - lam * (acc_ref[nb + cblk, :dv, :] / ls[nb + cblk]))
        o_ref[0, :, cblk * cols:(cblk + 1) * cols] = (
            d * _rms_scale(d) * sub_ref[...] * (1.0 - lam_init)).astype(BF16)


def _diff_attn(dqT, dk, dvT, lam_p, sub_g, *, lam_init, bounded):
    B, _, S = dqT.shape
    tiles = BOUNDED_TILES if bounded else ONLINE_TILES
    tq = _pick_tile(S, tiles.tq_diff)
    tk = _pick_tile(S, tiles.tk)
    cols = _pick_tile(tq, tiles.cols)
    hv = DIFF_V_DIM
    return pl.pallas_call(
        functools.partial(_diff_attn_kernel, tk=tk, lam_init=lam_init, bounded=bounded),
        grid=(B, DIFF_HEADS, S // tq),
        in_specs=[pl.BlockSpec((1, 2 * DIFF_HEAD_DIM, tq), lambda b, h, i: (b, h, i)),
                  pl.BlockSpec((1, S, 2 * DIFF_HEAD_DIM), lambda b, h, i: (b, 0, h)),
                  pl.BlockSpec((1, hv, S), lambda b, h, i: (b, h, 0)),
                  _resident(lam_p.shape), _resident(sub_g.shape)],
        out_specs=pl.BlockSpec((1, hv, tq), lambda b, h, i: (b, h, i)),
        out_shape=jax.ShapeDtypeStruct((B, DIFF_WIDTH, S), BF16),
        scratch_shapes=_attn_scratch(2 * (tq // cols), 2 * DIFF_HEAD_DIM, hv, tk, cols, bounded),
        compiler_params=_params("parallel", "parallel", "parallel"),
        name="diff_attn_bounded" if bounded else "diff_attn",
    )(dqT, dk, dvT, lam_p, sub_g)


def _gqa_attn_kernel(q_ref, k_ref, v_ref, o_ref, qs_ref, stage_ref, acc_ref, *, tk, bounded):
    d = GQA_HEAD_DIM
    tq = q_ref.shape[2]
    hp = qs_ref.shape[2] // tq
    zeros = jnp.zeros((d, hp * tq), BF16)
    v_rows = []
    for g in range(GQA_HEADS // hp):
        kv = (g * hp) // GQA_GROUP
        qh = jnp.concatenate([q_ref[0, j * d:(j + 1) * d, :] for j in range(g * hp, (g + 1) * hp)], axis=1)
        qs_ref[g] = jnp.concatenate([qh, zeros] if kv == 0 else [zeros, qh], axis=0)
        v_rows.append((kv * d, (kv + 1) * d))

    ls = _key_loop(k_ref, v_ref, qs_ref, stage_ref, acc_ref, v_rows, tk, bounded)
    for g in range(GQA_HEADS // hp):
        o = acc_ref[g, :d, :] / ls[g]
        for r in range(hp):
            j = g * hp + r
            o_ref[0, j * d:(j + 1) * d, :] = o[:, r * tq:(r + 1) * tq].astype(BF16)


def _gqa_attn(gqT, gk, gvT, *, bounded):
    B, _, S = gqT.shape
    tiles = BOUNDED_TILES if bounded else ONLINE_TILES
    tq = _pick_tile(S, tiles.tq_gqa)
    tk = _pick_tile(S, tiles.tk)
    kvw = GQA_KV_HEADS * GQA_HEAD_DIM
    hp = max(1, tiles.cols // tq)
    assert GQA_GROUP % hp == 0
    return pl.pallas_call(
        functools.partial(_gqa_attn_kernel, tk=tk, bounded=bounded),
        grid=(B, S // tq),
        in_specs=[pl.BlockSpec((1, GQA_WIDTH, tq), lambda b, i: (b, 0, i)),
                  pl.BlockSpec((1, S, kvw), lambda b, i: (b, 0, 0)),
                  pl.BlockSpec((1, kvw, S), lambda b, i: (b, 0, 0))],
        out_specs=pl.BlockSpec((1, GQA_WIDTH, tq), lambda b, i: (b, 0, i)),
        out_shape=jax.ShapeDtypeStruct((B, GQA_WIDTH, S), BF16),
        scratch_shapes=_attn_scratch(GQA_HEADS // hp, kvw, GQA_HEAD_DIM, tk, hp * tq, bounded),
        compiler_params=_params("parallel", "parallel"),
        name="gqa_attn_bounded" if bounded else "gqa_attn",
    )(gqT, gk, gvT)


def _score_bound(q_gain, k_gain, q_scale):
    return 1.02 * q_scale * HEAD * jnp.max(jnp.abs(q_gain)) * jnp.max(jnp.abs(k_gain))


def _mix_out_kernel(x_ref, a_ref, b_ref, gates_ref, wa_ref, wb_ref, wo_ref, o_ref):
    D = x_ref.shape[1]
    a = _dot(wa_ref[...], a_ref[0])
    b = _dot(wb_ref[...], b_ref[0])
    merged = gates_ref[0, :D, :].astype(F32) * a + gates_ref[0, D:, :].astype(F32) * b
    o_ref[0] = x_ref[0] + _dot(wo_ref[...], merged.astype(BF16))


def _mix_out(xT, aT, bT, gatesT, waT, wbT, woT, *, ts_pref=1024):
    B, D, S = xT.shape
    ts = _pick_tile(S, ts_pref)

    def tr(rows):
        return pl.BlockSpec((1, rows, ts), lambda b, i: (b, 0, i))

    return pl.pallas_call(
        _mix_out_kernel,
        grid=(B, S // ts),
        in_specs=[tr(D), tr(aT.shape[1]), tr(bT.shape[1]), tr(gatesT.shape[1]),
                  _resident(waT.shape), _resident(wbT.shape), _resident(woT.shape)],
        out_specs=tr(D),
        out_shape=jax.ShapeDtypeStruct((B, D, S), F32),
        compiler_params=_params("parallel", "parallel"),
        name="mix_out",
    )(xT, aT, bT, gatesT, waT, wbT, woT)


def _inv_freq(dim):
    return ROPE_THETA ** (-jnp.arange(0, dim, 2, dtype=F32) / dim)


def _rope_tables(S):
    t = jnp.arange(S, dtype=F32)
    row = jnp.floor_divide(jnp.arange(S), GRID_W).astype(F32)
    col = jnp.remainder(jnp.arange(S), GRID_W).astype(F32)
    ang_t = (t[:, None] * _inv_freq(DIFF_HEAD_DIM)[None, :]).T
    ang_r = (row[:, None] * _inv_freq(GQA_AXIS_DIM)[None, :]).T
    ang_c = (col[:, None] * _inv_freq(GQA_AXIS_DIM)[None, :]).T
    cos_d = jnp.concatenate([jnp.cos(ang_t)] * 2, axis=0)
    sin_d = jnp.concatenate([-jnp.sin(ang_t), jnp.sin(ang_t)], axis=0)
    cos_a = jnp.concatenate([jnp.cos(ang_r)] * 2 + [jnp.cos(ang_c)] * 2, axis=0)
    sin_a = jnp.concatenate([-jnp.sin(ang_r), jnp.sin(ang_r), -jnp.sin(ang_c), jnp.sin(ang_c)], axis=0)
    return (cos_d, sin_d), (cos_a, sin_a)


def _gain_table(gain, cos_sin, swap, scale):
    cos, sin = cos_sin
    g = gain.astype(F32)[:, None] * scale
    return jnp.stack([g * cos, swap(g) * sin], axis=0)


def _layer(xT, S, p, lam_init, ropes, first, last):
    (ffn1_norm, ffn1_w_gu, ffn1_w_down, mix_norm, w_in, diff_q_norm, diff_k_norm,
     lambda_q1, lambda_k1, lambda_q2, lambda_k2, diff_subln, gqa_q_norm, gqa_k_norm,
     w_up_diff, w_up_gqa, w_out, ffn2_norm, ffn2_w_gu, ffn2_w_down) = p
    rope_d, rope_a = ropes

    def col(v):
        return v.astype(F32)[:, None]

    def ffn_weights(w_gu, w_down):
        f = w_down.shape[0]
        return (w_gu[:, :f].T.astype(BF16), w_gu[:, f:].T.astype(BF16), w_down.T.astype(BF16))

    xT = _ffn(xT, col(ffn1_norm), *ffn_weights(ffn1_w_gu, ffn1_w_down), in_natural=first)

    diff_scale = LOG2E / math.sqrt(DIFF_HEAD_DIM)
    gqa_scale = LOG2E / math.sqrt(GQA_HEAD_DIM)
    tabs = (_gain_table(diff_q_norm, rope_d, _swap_diff, diff_scale),
            _gain_table(diff_k_norm, rope_d, _swap_diff, 1.0),
            _gain_table(gqa_q_norm, rope_a, _swap_axial, gqa_scale),
            _gain_table(gqa_k_norm, rope_a, _swap_axial, 1.0))
    dqT, dk, dvT, gqT, gk, gvT, gatesT = _mix_in(xT, col(mix_norm), w_in.T.astype(BF16), tabs)

    lam_p = jnp.stack([lambda_q1, lambda_k1, lambda_q2, lambda_k2], axis=0).astype(F32)
    aT = lax.cond(_score_bound(diff_q_norm, diff_k_norm, diff_scale) <= EXP2_SAFE_RANGE,
                  functools.partial(_diff_attn, lam_init=lam_init, bounded=True),
                  functools.partial(_diff_attn, lam_init=lam_init, bounded=False),
                  dqT, dk, dvT, lam_p, col(diff_subln))
    bT = lax.cond(_score_bound(gqa_q_norm, gqa_k_norm, gqa_scale) <= EXP2_SAFE_RANGE,
                  functools.partial(_gqa_attn, bounded=True),
                  functools.partial(_gqa_attn, bounded=False),
                  gqT, gk, gvT)
    xT = _mix_out(xT, aT, bT, gatesT, w_up_diff.T.astype(BF16), w_up_gqa.T.astype(BF16),
                  w_out.T.astype(BF16))
    return _ffn(xT, col(ffn2_norm), *ffn_weights(ffn2_w_gu, ffn2_w_down), out_natural=last)


def kernel(x_prompt, x_sample, ffn1_norm, ffn1_w_gu, ffn1_w_down, mix_norm, w_in, diff_q_norm, diff_k_norm, lambda_q1, lambda_k1, lambda_q2, lambda_k2, diff_subln, gqa_q_norm, gqa_k_norm, w_up_diff, w_up_gqa, w_out, ffn2_norm, ffn2_w_gu, ffn2_w_down):
    stacked = (ffn1_norm, ffn1_w_gu, ffn1_w_down, mix_norm, w_in, diff_q_norm, diff_k_norm,
               lambda_q1, lambda_k1, lambda_q2, lambda_k2, diff_subln, gqa_q_norm, gqa_k_norm,
               w_up_diff, w_up_gqa, w_out, ffn2_norm, ffn2_w_gu, ffn2_w_down)
    depth = ffn1_norm.shape[0]
    outs = []
    for x in (x_prompt, x_sample):
        S = x.shape[1]
        ropes = _rope_tables(S)
        y = x
        for l in range(depth):
            lam_init = 0.8 - 0.6 * math.exp(-0.3 * l)
            y = _layer(y, S, tuple(w[l] for w in stacked), lam_init, ropes,
                       first=(l == 0), last=(l == depth - 1))
        outs.append(y)
    return tuple(outs)
```

```python
import functools
import math
from typing import NamedTuple

import jax
import jax.numpy as jnp
from jax import lax
from jax.experimental import pallas as pl
from jax.experimental.pallas import tpu as pltpu

GRID_W = 64
ROPE_THETA = 10000.0
EPS = 1e-6
DIFF_HEADS = 4
DIFF_HEAD_DIM = 64
DIFF_V_DIM = 2 * DIFF_HEAD_DIM
DIFF_WIDTH = DIFF_HEADS * DIFF_V_DIM
GQA_HEADS = 8
GQA_KV_HEADS = 2
GQA_GROUP = GQA_HEADS // GQA_KV_HEADS
GQA_HEAD_DIM = 64
GQA_AXIS_DIM = GQA_HEAD_DIM // 2
GQA_WIDTH = GQA_HEADS * GQA_HEAD_DIM
HEAD = 64
LOG2E = math.log2(math.e)

SEC_DQ = 0
SEC_DK = SEC_DQ + DIFF_HEADS * 2 * DIFF_HEAD_DIM
SEC_DV = SEC_DK + DIFF_HEADS * 2 * DIFF_HEAD_DIM
SEC_GQ = SEC_DV + DIFF_WIDTH
SEC_GK = SEC_GQ + GQA_WIDTH
SEC_GV = SEC_GK + GQA_KV_HEADS * GQA_HEAD_DIM
SEC_GATES = SEC_GV + GQA_KV_HEADS * GQA_HEAD_DIM

VMEM_LIMIT_BYTES = 56 * 1024 * 1024

BF16 = jnp.bfloat16
F32 = jnp.float32


def _pick_tile(n, pref):
    t = min(n, pref)
    while n % t:
        t //= 2
    return t


def _params(*sem):
    return pltpu.CompilerParams(dimension_semantics=sem, vmem_limit_bytes=VMEM_LIMIT_BYTES)


def _resident(shape):
    nd = len(shape)
    return pl.BlockSpec(shape, lambda *_: (0,) * nd, pipeline_mode=pl.Buffered(1))


def _dot(a, b):
    return jnp.dot(a, b, preferred_element_type=F32)


def _rms_scale(x):
    return lax.rsqrt(jnp.mean(x * x, axis=0, keepdims=True) + EPS)


def _ffn_kernel(x_ref, g_ref, wg_ref, wu_ref, wd_ref, o_ref, *, in_natural, out_natural):
    x = x_ref[0]
    if in_natural:
        x = x.T
    h = (x * _rms_scale(x) * g_ref[...]).astype(BF16)
    gate = _dot(wg_ref[...], h)
    up = _dot(wu_ref[...], h)
    act = (gate * (1.0 / (1.0 + jnp.exp(-gate))) * up).astype(BF16)
    y = x + 0.5 * _dot(wd_ref[...], act)
    o_ref[0] = y.T if out_natural else y


def _ffn(x, g, wgT, wuT, wdT, *, in_natural=False, out_natural=False, ts_pref=512):
    if in_natural:
        B, S, D = x.shape
    else:
        B, D, S = x.shape
    F = wgT.shape[0]
    ts = _pick_tile(S, ts_pref)
    nat = pl.BlockSpec((1, ts, D), lambda b, i: (b, i, 0))
    tr = pl.BlockSpec((1, D, ts), lambda b, i: (b, 0, i))
    out_shape = (B, S, D) if out_natural else (B, D, S)
    return pl.pallas_call(
        functools.partial(_ffn_kernel, in_natural=in_natural, out_natural=out_natural),
        grid=(B, S // ts),
        in_specs=[nat if in_natural else tr, _resident((D, 1)),
                  _resident((F, D)), _resident((F, D)), _resident((D, F))],
        out_specs=nat if out_natural else tr,
        out_shape=jax.ShapeDtypeStruct(out_shape, F32),
        compiler_params=_params("parallel", "parallel"),
        name="ffn",
    )(x, g, wgT, wuT, wdT)


def _swap_diff(x):
    return jnp.concatenate([x[HEAD // 2:], x[:HEAD // 2]], axis=0)


def _swap_axial(x):
    q = HEAD // 4
    return jnp.concatenate([x[q:2 * q], x[:q], x[3 * q:], x[2 * q:3 * q]], axis=0)


def _norm_rope(u, n_groups, tab_ref, swap):
    ta = tab_ref[0]
    tb = tab_ref[1]
    outs = []
    for j in range(n_groups):
        x = u[j * HEAD:(j + 1) * HEAD]
        outs.append(_rms_scale(x) * (x * ta + swap(x) * tb))
    return jnp.concatenate(outs, axis=0)


def _mix_in_kernel(x_ref, g_ref, w_ref, tdq_ref, tdk_ref, tgq_ref, tgk_ref,
                   dq_ref, dk_ref, dv_ref, gq_ref, gk_ref, gv_ref, gates_ref):
    x = x_ref[0]
    h = (x * _rms_scale(x) * g_ref[...]).astype(BF16)

    def proj(lo, hi):
        return _dot(w_ref[lo:hi, :], h)

    n_dqk = DIFF_HEADS * 2
    dq_ref[0] = _norm_rope(proj(SEC_DQ, SEC_DK), n_dqk, tdq_ref, _swap_diff).astype(BF16)
    dk_ref[0] = _norm_rope(proj(SEC_DK, SEC_DV), n_dqk, tdk_ref, _swap_diff).T.astype(BF16)
    dv_ref[0] = proj(SEC_DV, SEC_GQ).astype(BF16)
    gq_ref[0] = _norm_rope(proj(SEC_GQ, SEC_GK), GQA_HEADS, tgq_ref, _swap_axial).astype(BF16)
    gk_ref[0] = _norm_rope(proj(SEC_GK, SEC_GV), GQA_KV_HEADS, tgk_ref, _swap_axial).T.astype(BF16)
    gv_ref[0] = proj(SEC_GV, SEC_GATES).astype(BF16)
    z = proj(SEC_GATES, w_ref.shape[0])
    gates_ref[0] = (1.0 / (1.0 + jnp.exp(-z))).astype(BF16)


def _mix_in(xT, g, w_inT, tabs, *, ts_pref=256):
    B, D, S = xT.shape
    d_in = w_inT.shape[0]
    ts = _pick_tile(S, ts_pref)
    n_gates = d_in - SEC_GATES
    kvw = GQA_KV_HEADS * GQA_HEAD_DIM

    def tr(rows):
        return pl.BlockSpec((1, rows, ts), lambda b, i: (b, 0, i))

    def nat(cols):
        return pl.BlockSpec((1, ts, cols), lambda b, i: (b, i, 0))

    tab = pl.BlockSpec((2, HEAD, ts), lambda b, i: (0, 0, i))
    sds = jax.ShapeDtypeStruct
    return pl.pallas_call(
        _mix_in_kernel,
        grid=(B, S // ts),
        in_specs=[tr(D), _resident((D, 1)), _resident((d_in, D)), tab, tab, tab, tab],
        out_specs=[tr(SEC_DK - SEC_DQ), nat(SEC_DV - SEC_DK), tr(DIFF_WIDTH),
                   tr(GQA_WIDTH), nat(kvw), tr(kvw), tr(n_gates)],
        out_shape=[sds((B, SEC_DK - SEC_DQ, S), BF16), sds((B, S, SEC_DV - SEC_DK), BF16),
                   sds((B, DIFF_WIDTH, S), BF16), sds((B, GQA_WIDTH, S), BF16),
                   sds((B, S, kvw), BF16), sds((B, kvw, S), BF16), sds((B, n_gates, S), BF16)],
        compiler_params=_params("parallel", "parallel"),
        name="mix_in",
    )(xT, g, w_inT, *tabs)


SUM_ROWS = 16
EXP2_SAFE_RANGE = 64.0


class _AttnTiles(NamedTuple):
    cols: int
    tk: int
    tq_diff: int
    tq_gqa: int
    trip_units: int


ONLINE_TILES = _AttnTiles(cols=512, tk=512, tq_diff=1024, tq_gqa=512, trip_units=16)
BOUNDED_TILES = _AttnTiles(cols=2048, tk=1024, tq_diff=2048, tq_gqa=512, trip_units=4)


def _fold_rows(x):
    slabs = [x[r:r + 8] for r in range(0, x.shape[0], 8)]
    while len(slabs) > 1:
        slabs = [a + b for a, b in zip(slabs[::2], slabs[1::2])]
    return slabs[0]


def _key_loop(k_ref, v_ref, qs_ref, stage_ref, acc_ref, v_rows, tk, bounded):
    n_tiles = k_ref.shape[1] // tk
    assert n_tiles % 2 == 0
    G = qs_ref.shape[0]
    trip_units = (BOUNDED_TILES if bounded else ONLINE_TILES).trip_units
    dv = acc_ref.shape[1] - (0 if bounded else SUM_ROWS)
    ones_rows = jnp.ones((SUM_ROWS, tk), BF16)

    def key_rows(kt):
        return kt * tk if isinstance(kt, int) else pl.multiple_of(kt * tk, tk)

    def produce(kt, buf, g):
        s = _dot(k_ref[0, pl.ds(key_rows(kt), tk), :], qs_ref[g])
        if bounded:
            p = jnp.exp2(s)
            stage_ref[buf, g] = p.astype(BF16)
            return _fold_rows(p)
        stage_ref[buf, g] = s
        return jnp.max(s, axis=0, keepdims=True)

    def step(kt, buf, state, stat, first=False, prefetch=True):
        vblk = v_ref[0, :, pl.ds(key_rows(kt), tk)]
        new_state, next_stat = [], []
        for g in range(G):
            if prefetch:
                next_stat.append(produce(kt + 1, 1 - buf, g))
            lo, hi = v_rows[g]
            if bounded:
                pv = _dot(vblk[lo:hi], stage_ref[buf, g])
                acc_ref[g] = pv if first else acc_ref[g] + pv
                new_state.append(stat[g] if first else state[g] + stat[g])
                continue
            v_ext = jnp.concatenate([vblk[lo:hi], ones_rows], axis=0)
            if first:
                m_new = stat[g]
                acc_ref[g] = _dot(v_ext, jnp.exp2(stage_ref[buf, g] - m_new).astype(BF16))
            else:
                m_new = jnp.maximum(state[g], stat[g])
                alpha = jnp.exp2(state[g] - m_new)
                p = jnp.exp2(stage_ref[buf, g] - m_new).astype(BF16)
                acc_ref[g] = alpha * acc_ref[g] + _dot(v_ext, p)
            new_state.append(m_new)
        return tuple(new_state), tuple(next_stat)

    def pair(kt, state, stat, first=False, last=False):
        state, stat = step(kt, 0, state, stat, first=first)
        return step(kt + 1, 1, state, stat, prefetch=not last)

    carry = pair(0, None, tuple(produce(0, 0, g) for g in range(G)), first=True, last=n_tiles == 2)
    if n_tiles > 4:
        carry = lax.fori_loop(1, n_tiles // 2 - 1, lambda u, c: pair(2 * u, *c), carry,
                              unroll=max(1, trip_units // (2 * G)))
    if n_tiles > 2:
        carry = pair(n_tiles - 2, *carry, last=True)
    if bounded:
        return [jnp.sum(part, axis=0, keepdims=True) for part in carry[0]]
    return [acc_ref[g, dv:dv + 1, :] for g in range(G)]


def _attn_scratch(n_groups, q_rows, dv, tk, cols, bounded):
    return [pltpu.VMEM((n_groups, q_rows, cols), BF16),
            pltpu.VMEM((2, n_groups, tk, cols), BF16 if bounded else F32),
            pltpu.VMEM((n_groups, dv + (0 if bounded else SUM_ROWS), cols), F32)]


def _diff_attn_kernel(q_ref, k_ref, v_ref, lam_ref, sub_ref, o_ref, qs_ref, stage_ref, acc_ref, *,
                      tk, lam_init, bounded):
    tq = q_ref.shape[2]
    cols = qs_ref.shape[2]
    nb = tq // cols
    q = q_ref[0]
    first_half = lax.broadcasted_iota(jnp.int32, q.shape, 0) < HEAD
    zero = jnp.zeros_like(q)
    q_maps = (jnp.where(first_half, q, zero), jnp.where(first_half, zero, q))
    for mp in range(2):
        for cblk in range(nb):
            qs_ref[mp * nb + cblk] = q_maps[mp][:, cblk * cols:(cblk + 1) * cols]

    dv = v_ref.shape[1]
    ls = _key_loop(k_ref, v_ref, qs_ref, stage_ref, acc_ref, [(0, dv)] * (2 * nb), tk, bounded)

    lam_p = lam_ref[...]
    lam = (jnp.exp(jnp.sum(lam_p[0:1] * lam_p[1:2], axis=1, keepdims=True))
           - jnp.exp(jnp.sum(lam_p[2:3] * lam_p[3:4], axis=1, keepdims=True)) + lam_init)
    for cblk in range(nb):
        d = (acc_ref[cblk, :dv, :] / ls[cblk]
             - lam * (acc_ref[nb + cblk, :dv, :] / ls[nb + cblk]))
        o_ref[0, :, cblk * cols:(cblk + 1) * cols] = (
            d * _rms_scale(d) * sub_ref[...] * (1.0 - lam_init)).astype(BF16)


def _diff_attn(dqT, dk, dvT, lam_p, sub_g, *, lam_init, bounded):
    B, _, S = dqT.shape
    tiles = BOUNDED_TILES if bounded else ONLINE_TILES
    tq = _pick_tile(S, tiles.tq_diff)
    tk = _pick_tile(S, tiles.tk)
    cols = _pick_tile(tq, tiles.cols)
    hv = DIFF_V_DIM
    return pl.pallas_call(
        functools.partial(_diff_attn_kernel, tk=tk, lam_init=lam_init, bounded=bounded),
        grid=(B, DIFF_HEADS, S // tq),
        in_specs=[pl.BlockSpec((1, 2 * DIFF_HEAD_DIM, tq), lambda b, h, i: (b, h, i)),
                  pl.BlockSpec((1, S, 2 * DIFF_HEAD_DIM), lambda b, h, i: (b, 0, h)),
                  pl.BlockSpec((1, hv, S), lambda b, h, i: (b, h, 0)),
                  _resident(lam_p.shape), _resident(sub_g.shape)],
        out_specs=pl.BlockSpec((1, hv, tq), lambda b, h, i: (b, h, i)),
        out_shape=jax.ShapeDtypeStruct((B, DIFF_WIDTH, S), BF16),
        scratch_shapes=_attn_scratch(2 * (tq // cols), 2 * DIFF_HEAD_DIM, hv, tk, cols, bounded),
        compiler_params=_params("parallel", "parallel", "parallel"),
        name="diff_attn_bounded" if bounded else "diff_attn",
    )(dqT, dk, dvT, lam_p, sub_g)


def _gqa_attn_kernel(q_ref, k_ref, v_ref, o_ref, qs_ref, stage_ref, acc_ref, *, tk, bounded):
    d = GQA_HEAD_DIM
    tq = q_ref.shape[2]
    hp = qs_ref.shape[2] // tq
    zeros = jnp.zeros((d, hp * tq), BF16)
    v_rows = []
    for g in range(GQA_HEADS // hp):
        kv = (g * hp) // GQA_GROUP
        qh = jnp.concatenate([q_ref[0, j * d:(j + 1) * d, :] for j in range(g * hp, (g + 1) * hp)], axis=1)
        qs_ref[g] = jnp.concatenate([qh, zeros] if kv == 0 else [zeros, qh], axis=0)
        v_rows.append((kv * d, (kv + 1) * d))
    out_rows = v_rows
    if bounded:
        v_rows = [(0, v_ref.shape[1])] * len(v_rows)

    ls = _key_loop(k_ref, v_ref, qs_ref, stage_ref, acc_ref, v_rows, tk, bounded)
    for g in range(GQA_HEADS // hp):
        lo = out_rows[g][0] - v_rows[g][0]
        o = acc_ref[g, lo:lo + d, :] / ls[g]
        for r in range(hp):
            j = g * hp + r
            o_ref[0, j * d:(j + 1) * d, :] = o[:, r * tq:(r + 1) * tq].astype(BF16)


def _gqa_attn(gqT, gk, gvT, *, bounded):
    B, _, S = gqT.shape
    tiles = BOUNDED_TILES if bounded else ONLINE_TILES
    tq = _pick_tile(S, tiles.tq_gqa)
    tk = _pick_tile(S, tiles.tk)
    kvw = GQA_KV_HEADS * GQA_HEAD_DIM
    hp = max(1, tiles.cols // tq)
    assert GQA_GROUP % hp == 0
    return pl.pallas_call(
        functools.partial(_gqa_attn_kernel, tk=tk, bounded=bounded),
        grid=(B, S // tq),
        in_specs=[pl.BlockSpec((1, GQA_WIDTH, tq), lambda b, i: (b, 0, i)),
                  pl.BlockSpec((1, S, kvw), lambda b, i: (b, 0, 0)),
                  pl.BlockSpec((1, kvw, S), lambda b, i: (b, 0, 0))],
        out_specs=pl.BlockSpec((1, GQA_WIDTH, tq), lambda b, i: (b, 0, i)),
        out_shape=jax.ShapeDtypeStruct((B, GQA_WIDTH, S), BF16),
        scratch_shapes=_attn_scratch(GQA_HEADS // hp, kvw, kvw if bounded else GQA_HEAD_DIM, tk,
                                     hp * tq, bounded),
        compiler_params=_params("parallel", "parallel"),
        name="gqa_attn_bounded" if bounded else "gqa_attn",
    )(gqT, gk, gvT)


def _score_bound(q_gain, k_gain, q_scale):
    return 1.02 * q_scale * HEAD * jnp.max(jnp.abs(q_gain)) * jnp.max(jnp.abs(k_gain))


def _mix_out_kernel(x_ref, a_ref, b_ref, gates_ref, wa_ref, wb_ref, wo_ref, o_ref):
    D = x_ref.shape[1]
    a = _dot(wa_ref[...], a_ref[0])
    b = _dot(wb_ref[...], b_ref[0])
    merged = gates_ref[0, :D, :].astype(F32) * a + gates_ref[0, D:, :].astype(F32) * b
    o_ref[0] = x_ref[0] + _dot(wo_ref[...], merged.astype(BF16))


def _mix_out(xT, aT, bT, gatesT, waT, wbT, woT, *, ts_pref=1024):
    B, D, S = xT.shape
    ts = _pick_tile(S, ts_pref)

    def tr(rows):
        return pl.BlockSpec((1, rows, ts), lambda b, i: (b, 0, i))

    return pl.pallas_call(
        _mix_out_kernel,
        grid=(B, S // ts),
        in_specs=[tr(D), tr(aT.shape[1]), tr(bT.shape[1]), tr(gatesT.shape[1]),
                  _resident(waT.shape), _resident(wbT.shape), _resident(woT.shape)],
        out_specs=tr(D),
        out_shape=jax.ShapeDtypeStruct((B, D, S), F32),
        compiler_params=_params("parallel", "parallel"),
        name="mix_out",
    )(xT, aT, bT, gatesT, waT, wbT, woT)


def _inv_freq(dim):
    return ROPE_THETA ** (-jnp.arange(0, dim, 2, dtype=F32) / dim)


def _rope_tables(S):
    t = jnp.arange(S, dtype=F32)
    row = jnp.floor_divide(jnp.arange(S), GRID_W).astype(F32)
    col = jnp.remainder(jnp.arange(S), GRID_W).astype(F32)
    ang_t = (t[:, None] * _inv_freq(DIFF_HEAD_DIM)[None, :]).T
    ang_r = (row[:, None] * _inv_freq(GQA_AXIS_DIM)[None, :]).T
    ang_c = (col[:, None] * _inv_freq(GQA_AXIS_DIM)[None, :]).T
    cos_d = jnp.concatenate([jnp.cos(ang_t)] * 2, axis=0)
    sin_d = jnp.concatenate([-jnp.sin(ang_t), jnp.sin(ang_t)], axis=0)
    cos_a = jnp.concatenate([jnp.cos(ang_r)] * 2 + [jnp.cos(ang_c)] * 2, axis=0)
    sin_a = jnp.concatenate([-jnp.sin(ang_r), jnp.sin(ang_r), -jnp.sin(ang_c), jnp.sin(ang_c)], axis=0)
    return (cos_d, sin_d), (cos_a, sin_a)


def _gain_table(gain, cos_sin, swap, scale):
    cos, sin = cos_sin
    g = gain.astype(F32)[:, None] * scale
    return jnp.stack([g * cos, swap(g) * sin], axis=0)


def _layer(xT, S, p, lam_init, ropes, first, last):
    (ffn1_norm, ffn1_w_gu, ffn1_w_down, mix_norm, w_in, diff_q_norm, diff_k_norm,
     lambda_q1, lambda_k1, lambda_q2, lambda_k2, diff_subln, gqa_q_norm, gqa_k_norm,
     w_up_diff, w_up_gqa, w_out, ffn2_norm, ffn2_w_gu, ffn2_w_down) = p
    rope_d, rope_a = ropes

    def col(v):
        return v.astype(F32)[:, None]

    def ffn_weights(w_gu, w_down):
        f = w_down.shape[0]
        return (w_gu[:, :f].T.astype(BF16), w_gu[:, f:].T.astype(BF16), w_down.T.astype(BF16))

    xT = _ffn(xT, col(ffn1_norm), *ffn_weights(ffn1_w_gu, ffn1_w_down), in_natural=first)

    diff_scale = LOG2E / math.sqrt(DIFF_HEAD_DIM)
    gqa_scale = LOG2E / math.sqrt(GQA_HEAD_DIM)
    tabs = (_gain_table(diff_q_norm, rope_d, _swap_diff, diff_scale),
            _gain_table(diff_k_norm, rope_d, _swap_diff, 1.0),
            _gain_table(gqa_q_norm, rope_a, _swap_axial, gqa_scale),
            _gain_table(gqa_k_norm, rope_a, _swap_axial, 1.0))
    dqT, dk, dvT, gqT, gk, gvT, gatesT = _mix_in(xT, col(mix_norm), w_in.T.astype(BF16), tabs)

    lam_p = jnp.stack([lambda_q1, lambda_k1, lambda_q2, lambda_k2], axis=0).astype(F32)
    aT = lax.cond(_score_bound(diff_q_norm, diff_k_norm, diff_scale) <= EXP2_SAFE_RANGE,
                  functools.partial(_diff_attn, lam_init=lam_init, bounded=True),
                  functools.partial(_diff_attn, lam_init=lam_init, bounded=False),
                  dqT, dk, dvT, lam_p, col(diff_subln))
    bT = lax.cond(_score_bound(gqa_q_norm, gqa_k_norm, gqa_scale) <= EXP2_SAFE_RANGE,
                  functools.partial(_gqa_attn, bounded=True),
                  functools.partial(_gqa_attn, bounded=False),
                  gqT, gk, gvT)
    xT = _mix_out(xT, aT, bT, gatesT, w_up_diff.T.astype(BF16), w_up_gqa.T.astype(BF16),
                  w_out.T.astype(BF16))
    return _ffn(xT, col(ffn2_norm), *ffn_weights(ffn2_w_gu, ffn2_w_down), out_natural=last)


def kernel(x_prompt, x_sample, ffn1_norm, ffn1_w_gu, ffn1_w_down, mix_norm, w_in, diff_q_norm, diff_k_norm, lambda_q1, lambda_k1, lambda_q2, lambda_k2, diff_subln, gqa_q_norm, gqa_k_norm, w_up_diff, w_up_gqa, w_out, ffn2_norm, ffn2_w_gu, ffn2_w_down):
    stacked = (ffn1_norm, ffn1_w_gu, ffn1_w_down, mix_norm, w_in, diff_q_norm, diff_k_norm,
               lambda_q1, lambda_k1, lambda_q2, lambda_k2, diff_subln, gqa_q_norm, gqa_k_norm,
               w_up_diff, w_up_gqa, w_out, ffn2_norm, ffn2_w_gu, ffn2_w_down)
    depth = ffn1_norm.shape[0]
    outs = []
    for x in (x_prompt, x_sample):
        S = x.shape[1]
        ropes = _rope_tables(S)
        y = x
        for l in range(depth):
            lam_init = 0.8 - 0.6 * math.exp(-0.3 * l)
            y = _layer(y, S, tuple(w[l] for w in stacked), lam_init, ropes,
                       first=(l == 0), last=(l == depth - 1))
        outs.append(y)
    return tuple(outs)
```
